```python
import math
import jax, jax.numpy as jnp
from jax import lax
import numpy as np

D_MODEL = 1024
BATCH = 4
SEQ = 8192
DEPTH = 4

CHUNK = 64
N_MIXERS = 4
RMS_EPS = 1e-6

A_HEADS = 16
A_HEAD_DIM = D_MODEL // A_HEADS
A_WIDTH = A_HEADS * A_HEAD_DIM
A_LEFT_CHUNKS = 8
A_BAND = (A_LEFT_CHUNKS + 1) * CHUNK
A_MAX_REL = 256
A_REL_SIZE = (CHUNK - 1) + A_MAX_REL + 1

SG_FF = 6 * D_MODEL
SG_HALF = SG_FF // 2
SG_GROUPS = 8
SG_WINDOW = 128

GLA_HEADS = 4
GLA_KEY_WIDTH = D_MODEL // 2
GLA_VALUE_WIDTH = D_MODEL
GLA_DK = GLA_KEY_WIDTH // GLA_HEADS
GLA_DV = GLA_VALUE_WIDTH // GLA_HEADS
GLA_GATE_RANK = 16
GLA_TAU = 16.0
GLA_IN_WIDTH = 2 * GLA_KEY_WIDTH + 2 * GLA_VALUE_WIDTH + GLA_GATE_RANK

SB_HEADS = 16
SB_HEAD_DIM = D_MODEL // SB_HEADS
SB_WIDTH = SB_HEADS * SB_HEAD_DIM
SB_BLOCK = 128

FFN_HIDDEN = 4 * D_MODEL

kernel_name = "hybrid_chunk_causal_interleaved_trunk"


def rms_norm(x, gain):
    xf = x.astype(jnp.float32)
    y = xf * lax.rsqrt(jnp.mean(xf * xf, axis=-1, keepdims=True) + RMS_EPS)
    return (y * gain.astype(jnp.float32)).astype(x.dtype)


def chunked_relpos_attention(h, w_in, q_gain, k_gain, rel_bias, w_out):
    B, S, _ = h.shape
    nc = S // CHUNK
    q, k, v = jnp.split(h @ w_in, 3, axis=-1)
    q = rms_norm(q.reshape(B, S, A_HEADS, A_HEAD_DIM), q_gain) * (A_HEAD_DIM ** -0.5)
    k = rms_norm(k.reshape(B, S, A_HEADS, A_HEAD_DIM), k_gain)
    v = v.reshape(B, S, A_HEADS, A_HEAD_DIM)
    pad = A_LEFT_CHUNKS * CHUNK
    kp = jnp.pad(k, ((0, 0), (pad, 0), (0, 0), (0, 0)))
    vp = jnp.pad(v, ((0, 0), (pad, 0), (0, 0), (0, 0)))
    rel = jnp.arange(CHUNK)[:, None] + pad - jnp.arange(A_BAND)[None, :]
    bias_idx = jnp.clip(rel, -(CHUNK - 1), A_MAX_REL) + (CHUNK - 1)
    bias = rel_bias.astype(jnp.float32)[:, bias_idx]

    def one_chunk(ci):
        start = ci * CHUNK
        qc = lax.dynamic_slice_in_dim(q, start, CHUNK, axis=1)
        kb = lax.dynamic_slice_in_dim(kp, start, A_BAND, axis=1)
        vb = lax.dynamic_slice_in_dim(vp, start, A_BAND, axis=1)
        s = jnp.einsum('bihd,bjhd->bhij', qc, kb).astype(jnp.float32) + bias
        valid = (start + jnp.arange(A_BAND)) >= pad
        s = jnp.where(valid[None, None, None, :], s, -jnp.inf)
        p = jax.nn.softmax(s, axis=-1).astype(vb.dtype)
        return jnp.einsum('bhij,bjhd->bihd', p, vb)

    o = lax.map(one_chunk, jnp.arange(nc))
    o = o.transpose(1, 0, 2, 3, 4).reshape(B, S, A_WIDTH)
    return o @ w_out


def chunked_spatial_gating(h, w_in, v_gain, w_s, b_s, w_out):
    B, S, _ = h.shape
    nw = S // SG_WINDOW
    z = jax.nn.gelu(h @ w_in, approximate=False)
    u, v = jnp.split(z, 2, axis=-1)
    v = rms_norm(v, v_gain).reshape(B, nw, SG_WINDOW, SG_GROUPS, SG_HALF // SG_GROUPS)
    chunk_id = jnp.arange(SG_WINDOW) // CHUNK
    mask = chunk_id[:, None] >= chunk_id[None, :]
    w = jnp.where(mask[None], w_s, 0.0).astype(v.dtype)
    vm = jnp.einsum('gij,bnjgc->bnigc', w, v) + b_s.T.astype(v.dtype)[None, None, :, :, None]
    return (u * vm.reshape(B, S, SG_HALF)) @ w_out


def gated_linear_attention(h, w_in, w_gate_up, b_gate, o_gain, w_out):
    B, S, _ = h.shape
    nc = S // CHUNK
    splits = [GLA_KEY_WIDTH, 2 * GLA_KEY_WIDTH, 2 * GLA_KEY_WIDTH + GLA_VALUE_WIDTH,
              2 * GLA_KEY_WIDTH + 2 * GLA_VALUE_WIDTH]
    q, k, v, r, a = jnp.split(h @ w_in, splits, axis=-1)
    log_alpha = jax.nn.log_sigmoid((a @ w_gate_up + b_gate).astype(jnp.float32)) / GLA_TAU

    def to_chunks(t, d):
        return t.reshape(B, nc, CHUNK, GLA_HEADS, d).transpose(1, 0, 3, 2, 4).astype(jnp.float32)

    qc = to_chunks(q, GLA_DK) * (GLA_DK ** -0.5)
    kc = to_chunks(k, GLA_DK)
    vc = to_chunks(v, GLA_DV)
    bc = jnp.cumsum(to_chunks(log_alpha, GLA_DK), axis=3)
    causal = jnp.tril(jnp.ones((CHUNK, CHUNK), dtype=bool))

    def step(state, inp):
        qt, kt, vt, bt = inp
        o_inter = jnp.einsum('bhtk,bhkv->bhtv', qt * jnp.exp(bt), state)
        decay = bt[:, :, :, None, :] - bt[:, :, None, :, :]
        decay = jnp.exp(jnp.where(causal[None, None, :, :, None], decay, -jnp.inf))
        att = jnp.einsum('bhtk,bhsk,bhtsk->bhts', qt, kt, decay)
        o_intra = jnp.einsum('bhts,bhsv->bhtv', att, vt)
        b_last = bt[:, :, -1:, :]
        k_dec = kt * jnp.exp(b_last - bt)
        state = state * jnp.exp(b_last[:, :, 0, :])[..., None] + jnp.einsum('bhsk,bhsv->bhkv', k_dec, vt)
        return state, o_inter + o_intra

    state0 = jnp.zeros((B, GLA_HEADS, GLA_DK, GLA_DV), jnp.float32)
    _, o = lax.scan(step, state0, (qc, kc, vc, bc))
    o = o.transpose(1, 0, 3, 2, 4).reshape(B, S, GLA_HEADS, GLA_DV).astype(h.dtype)
    o = rms_norm(o, o_gain).reshape(B, S, GLA_VALUE_WIDTH)
    return (jax.nn.silu(r) * o) @ w_out


def stick_breaking_attention(h, w_in, w_out):
    B, S, _ = h.shape
    q, k, v = jnp.split(h @ w_in, 3, axis=-1)
    q = q.reshape(B, S, SB_HEADS, SB_HEAD_DIM).transpose(0, 2, 1, 3)
    k = k.reshape(B, S, SB_HEADS, SB_HEAD_DIM).transpose(0, 2, 1, 3)
    v = v.reshape(B, S, SB_HEADS, SB_HEAD_DIM).transpose(0, 2, 1, 3)
    scale = SB_HEAD_DIM ** -0.5
    outs = []
    for blk in range(S // SB_BLOCK):
        q0 = blk * SB_BLOCK
        kend = q0 + SB_BLOCK
        z = jnp.einsum('bhtd,bhsd->bhts', q[:, :, q0:kend], k[:, :, :kend]).astype(jnp.float32) * scale
        t_pos = q0 + jnp.arange(SB_BLOCK)
        strict = jnp.arange(kend)[None, :] < t_pos[:, None]
        log_not = jnp.where(strict, jax.nn.log_sigmoid(-z), 0.0)
        tail = lax.cumsum(log_not, axis=3, reverse=True)
        excl = jnp.concatenate([tail[..., 1:], jnp.zeros_like(tail[..., :1])], axis=-1)
        weights = jnp.where(strict, jnp.exp(jax.nn.log_sigmoid(z) + excl), 0.0)
        outs.append(jnp.einsum('bhts,bhsd->bhtd', weights.astype(v.dtype), v[:, :, :kend]))
    o = jnp.concatenate(outs, axis=2).transpose(0, 2, 1, 3).reshape(B, S, SB_WIDTH)
    return o @ w_out


def squared_relu_mlp(h, w1, w2):
    return jnp.square(jax.nn.relu(h @ w1)) @ w2


def _n_mixer_layers(m):
    return len(range(m, DEPTH, N_MIXERS))


def setup_inputs(seed: int = 0) -> dict:
    key = jax.random.key(seed)
    keys = iter(jax.random.split(key, 32))
    D = D_MODEL

    def nrm(shape, scale):
        return jax.random.normal(next(keys), shape, jnp.float32) * scale

    nA, nB, nC, nD = (_n_mixer_layers(m) for m in range(N_MIXERS))
    return {
        "x": nrm((BATCH, SEQ, D), 1.0),
        "c": nrm((BATCH, D), 1.0),
        "ada_w": nrm((DEPTH, D, 6 * D), 0.5 * D ** -0.5),
        "ada_b": nrm((DEPTH, 6 * D), 0.02),
        "norm_mix": 1.0 + nrm((DEPTH, D), 0.05),
        "norm_ffn": 1.0 + nrm((DEPTH, D), 0.05),
        "ffn_w1": nrm((DEPTH, D, FFN_HIDDEN), D ** -0.5),
        "ffn_w2": nrm((DEPTH, FFN_HIDDEN, D), FFN_HIDDEN ** -0.5),
        "a_w_in": nrm((nA, D, 3 * A_WIDTH), D ** -0.5),
        "a_q_gain": 1.0 + nrm((nA, A_HEAD_DIM), 0.05),
        "a_k_gain": 1.0 + nrm((nA, A_HEAD_DIM), 0.05),
        "a_rel_bias": nrm((nA, A_HEADS, A_REL_SIZE), 0.5),
        "a_w_out": nrm((nA, A_WIDTH, D), A_WIDTH ** -0.5),
        "b_w_in": nrm((nB, D, SG_FF), D ** -0.5),
        "b_v_gain": 1.0 + nrm((nB, SG_HALF), 0.05),
        "b_w_s": nrm((nB, SG_GROUPS, SG_WINDOW, SG_WINDOW), SG_WINDOW ** -0.5),
        "b_b_s": 1.0 + nrm((nB, SG_GROUPS, SG_WINDOW), 0.05),
        "b_w_out": nrm((nB, SG_HALF, D), SG_HALF ** -0.5),
        "c_w_in": nrm((nC, D, GLA_IN_WIDTH), D ** -0.5),
        "c_w_gate_up": nrm((nC, GLA_GATE_RANK, GLA_KEY_WIDTH), GLA_GATE_RANK ** -0.5),
        "c_b_gate": nrm((nC, GLA_KEY_WIDTH), 0.02),
        "c_o_gain": 1.0 + nrm((nC, GLA_DV), 0.05),
        "c_w_out": nrm((nC, GLA_VALUE_WIDTH, D), GLA_VALUE_WIDTH ** -0.5),
        "d_w_in": nrm((nD, D, 3 * SB_WIDTH), D ** -0.5),
        "d_w_out": nrm((nD, SB_WIDTH, D), SB_WIDTH ** -0.5),
    }


def reference(x, c, ada_w, ada_b, norm_mix, norm_ffn, ffn_w1, ffn_w2,
              a_w_in, a_q_gain, a_k_gain, a_rel_bias, a_w_out,
              b_w_in, b_v_gain, b_w_s, b_b_s, b_w_out,
              c_w_in, c_w_gate_up, c_b_gate, c_o_gain, c_w_out,
              d_w_in, d_w_out):
    cond = jax.nn.silu(c)
    for i in range(DEPTH):
        m, j = i % N_MIXERS, i // N_MIXERS
        mod = cond @ ada_w[i] + ada_b[i]
        sh1, sc1, g1, sh2, sc2, g2 = [t[:, None, :] for t in jnp.split(mod, 6, axis=-1)]
        h = rms_norm(x, norm_mix[i]) * (1.0 + sc1) + sh1
        if m == 0:
            y = chunked_relpos_attention(h, a_w_in[j], a_q_gain[j], a_k_gain[j], a_rel_bias[j], a_w_out[j])
        elif m == 1:
            y = chunked_spatial_gating(h, b_w_in[j], b_v_gain[j], b_w_s[j], b_b_s[j], b_w_out[j])
        elif m == 2:
            y = gated_linear_attention(h, c_w_in[j], c_w_gate_up[j], c_b_gate[j], c_o_gain[j], c_w_out[j])
        else:
            y = stick_breaking_attention(h, d_w_in[j], d_w_out[j])
        x = x + g1 * y
        h = rms_norm(x, norm_ffn[i]) * (1.0 + sc2) + sh2
        x = x + g2 * squared_relu_mlp(h, ffn_w1[i], ffn_w2[i])
    return x
```

```python
import functools

import jax
import jax.numpy as jnp
import numpy as np
from jax import lax
from jax.experimental import pallas as pl
from jax.experimental.pallas import tpu as pltpu

F32 = jnp.float32
BF16 = jnp.bfloat16

D_MODEL = 1024
DEPTH = 4
N_MIXERS = 4
CHUNK = 64
RMS_EPS = 1e-6

A_HEADS = 16
A_HEAD_DIM = 64
A_LEFT_CHUNKS = 8
A_MAX_REL = 256
A_BAND = (A_LEFT_CHUNKS + 1) * CHUNK
A_BAND_PAD = 640
A_GROUP = 8 * CHUNK

SG_FF = 6 * D_MODEL
SG_HALF = SG_FF // 2
SG_GROUPS = 8
SG_WINDOW = 128
SG_GROUP_WIDTH = SG_HALF // SG_GROUPS

GLA_HEADS = 4
GLA_KEY_WIDTH = D_MODEL // 2
GLA_DK = 128
GLA_DV = 256
GLA_GATE_RANK = 16
GLA_GATE_PAD = 128
GLA_TAU = 16.0
GLA_LEVELS = 6
GLA_STEP_CHUNKS = 8

SB_HEADS = 16
SB_HEAD_DIM = 64
SB_BLOCK = 256

FFN_HIDDEN = 4 * D_MODEL
FFN_SLAB = 1024

LANES = 128
HEAD_PAIR = LANES
VMEM_LIMIT_BYTES = 56 * 1024 * 1024

TM_PROJ = 512
TM_MIXB = 256

_NT = (((1,), (1,)), ((), ()))
_TN = (((0,), (0,)), ((), ()))


def _params(*semantics):
    return pltpu.CompilerParams(dimension_semantics=semantics,
                                vmem_limit_bytes=VMEM_LIMIT_BYTES)


def _dot(a, b):
    return jnp.dot(a, b, preferred_element_type=F32)


def _dot_nt(a, b):
    return lax.dot_general(a, b, _NT, preferred_element_type=F32)


def _resident(shape):
    zeros = (0,) * len(shape)
    return pl.BlockSpec(shape, lambda *_: zeros, pipeline_mode=pl.Buffered(1))


def _gelu(x):
    return 0.5 * x * (1.0 + lax.erf(x * (2.0 ** -0.5)))


def _modulated_norm(x, gain, scale, shift):
    ms = jnp.mean(x * x, axis=-1, keepdims=True)
    y = x * lax.rsqrt(ms + RMS_EPS) * gain
    return y * (1.0 + scale) + shift


def _mod_kernel(c_ref, w_ref, b_ref, o_ref):
    cond = jax.nn.silu(c_ref[...])
    o_ref[0] = jnp.dot(cond, w_ref[0], precision=lax.Precision.HIGHEST,
                       preferred_element_type=F32) + b_ref[0]


def _modulation(c, ada_w, ada_b):
    batch = c.shape[0]
    rows = 8
    c_pad = jnp.zeros((rows, D_MODEL), F32).at[:batch].set(c)
    tn = 1536
    out = pl.pallas_call(
        _mod_kernel,
        grid=(DEPTH, 6 * D_MODEL // tn),
        in_specs=[
            pl.BlockSpec((rows, D_MODEL), lambda i, j: (0, 0)),
            pl.BlockSpec((1, D_MODEL, tn), lambda i, j: (i, 0, j)),
            pl.BlockSpec((1, 1, tn), lambda i, j: (i, 0, j)),
        ],
        out_specs=pl.BlockSpec((1, rows, tn), lambda i, j: (i, 0, j)),
        out_shape=jax.ShapeDtypeStruct((DEPTH, rows, 6 * D_MODEL), F32),
        compiler_params=_params("parallel", "parallel"),
        name="adaln_modulation",
    )(c_pad, ada_w, ada_b.reshape(DEPTH, 1, 6 * D_MODEL))
    return out[:, :batch].reshape(DEPTH, batch, 6, D_MODEL)


def _normed_input(x_ref, mod_ref, gain_ref):
    shift, scale = mod_ref[0, 0:1, :], mod_ref[0, 1:2, :]
    return _modulated_norm(x_ref[...], gain_ref[...], scale, shift).astype(BF16)


def _inproj_a_kernel(x_ref, mod_ref, gain_ref, w_ref, qg_ref, kg_ref, seg_ref,
                     q_ref, k_ref, v_ref):
    h = _normed_input(x_ref, mod_ref, gain_ref)
    seg = seg_ref[...]

    def head_norm(y, gain):
        ssq = _dot((y * y).astype(BF16), seg)
        return y * lax.rsqrt(ssq * (1.0 / A_HEAD_DIM) + RMS_EPS) * gain

    width = seg.shape[0]
    for c in range(D_MODEL // width):
        cols = slice(c * width, (c + 1) * width)
        yq = _dot(h, w_ref[:, c * width:(c + 1) * width])
        q_ref[:, cols] = (head_norm(yq, qg_ref[:, cols]) * (A_HEAD_DIM ** -0.5)).astype(BF16)
        yk = _dot(h, w_ref[:, D_MODEL + c * width:D_MODEL + (c + 1) * width])
        k_ref[:, cols] = head_norm(yk, kg_ref[:, cols]).astype(BF16)
        yv = _dot(h, w_ref[:, 2 * D_MODEL + c * width:2 * D_MODEL + (c + 1) * width])
        v_ref[:, cols] = yv.astype(BF16)


def _inproj_d_kernel(x_ref, mod_ref, gain_ref, w_ref, q_ref, k_ref, v_ref):
    h = _normed_input(x_ref, mod_ref, gain_ref)
    width = 512
    for c in range(D_MODEL // width):
        cols = slice(c * width, (c + 1) * width)
        yq = _dot(h, w_ref[:, c * width:(c + 1) * width])
        q_ref[:, cols] = (yq * (SB_HEAD_DIM ** -0.5)).astype(BF16)
        yk = _dot(h, w_ref[:, D_MODEL + c * width:D_MODEL + (c + 1) * width])
        k_ref[:, cols] = yk.astype(BF16)
        yv = _dot(h, w_ref[:, 2 * D_MODEL + c * width:2 * D_MODEL + (c + 1) * width])
        v_ref[:, cols] = yv.astype(BF16)


def _inproj_c_kernel(x_ref, mod_ref, gain_ref, w_ref, wg_ref, bg_ref,
                     q_ref, k_ref, v_ref, r_ref, la_ref):
    h = _normed_input(x_ref, mod_ref, gain_ref)
    kw = GLA_KEY_WIDTH
    q_ref[...] = (_dot(h, w_ref[:, 0:kw]) * (GLA_DK ** -0.5)).astype(BF16)
    k_ref[...] = _dot(h, w_ref[:, kw:2 * kw]).astype(BF16)
    for c in range(2):
        cols = slice(c * kw, (c + 1) * kw)
        v_ref[:, cols] = _dot(h, w_ref[:, 2 * kw + c * kw:2 * kw + (c + 1) * kw]).astype(BF16)
        r_ref[:, cols] = _dot(h, w_ref[:, 4 * kw + c * kw:4 * kw + (c + 1) * kw]).astype(BF16)
    a = _dot(h, w_ref[:, 6 * kw:6 * kw + GLA_GATE_PAD]).astype(BF16)
    gate = _dot(a, wg_ref[...]) + bg_ref[...]
    la_ref[...] = jax.nn.log_sigmoid(gate) * (1.0 / GLA_TAU)


def _row_specs(tokens, seq, tm):
    tiles_per_batch = seq // tm
    x_spec = pl.BlockSpec((tm, D_MODEL), lambda i: (i, 0))
    mod_spec = pl.BlockSpec((1, 6, D_MODEL), lambda i: (i // tiles_per_batch, 0, 0))
    return (tokens // tm,), x_spec, mod_spec


def _row_out(tokens, width, tm, dtype=BF16):
    return (pl.BlockSpec((tm, width), lambda i: (i, 0)),
            jax.ShapeDtypeStruct((tokens, width), dtype))


def _inproj_a(x2, mod, gain, w_in, q_gain, k_gain, seq):
    tokens = x2.shape[0]
    grid, x_spec, mod_spec = _row_specs(tokens, seq, TM_PROJ)
    seg_width = 256
    seg_id = np.arange(seg_width) // A_HEAD_DIM
    seg = jnp.asarray(seg_id[:, None] == seg_id[None, :], BF16)
    specs, shapes = zip(*[_row_out(tokens, D_MODEL, TM_PROJ)] * 3)
    return pl.pallas_call(
        _inproj_a_kernel,
        grid=grid,
        in_specs=[x_spec, mod_spec, _resident((1, D_MODEL)), _resident((D_MODEL, 3 * D_MODEL)),
                  _resident((1, D_MODEL)), _resident((1, D_MODEL)),
                  _resident((seg_width, seg_width))],
        out_specs=list(specs), out_shape=list(shapes),
        compiler_params=_params("parallel"),
        name="inproj_relpos_attention",
    )(x2, mod, gain.reshape(1, D_MODEL), w_in.astype(BF16),
      jnp.tile(q_gain, A_HEADS).reshape(1, D_MODEL), jnp.tile(k_gain, A_HEADS).reshape(1, D_MODEL), seg)


def _inproj_d(x2, mod, gain, w_in, seq):
    tokens = x2.shape[0]
    grid, x_spec, mod_spec = _row_specs(tokens, seq, TM_PROJ)
    specs, shapes = zip(*[_row_out(tokens, D_MODEL, TM_PROJ)] * 3)
    return pl.pallas_call(
        _inproj_d_kernel,
        grid=grid,
        in_specs=[x_spec, mod_spec, _resident((1, D_MODEL)), _resident((D_MODEL, 3 * D_MODEL))],
        out_specs=list(specs), out_shape=list(shapes),
        compiler_params=_params("parallel"),
        name="inproj_stick_breaking",
    )(x2, mod, gain.reshape(1, D_MODEL), w_in.astype(BF16))


def _inproj_c(x2, mod, gain, w_in, w_gate_up, b_gate, seq):
    tokens = x2.shape[0]
    grid, x_spec, mod_spec = _row_specs(tokens, seq, TM_PROJ)
    main = 2 * GLA_KEY_WIDTH + 2 * D_MODEL
    w_pad = jnp.zeros((D_MODEL, main + GLA_GATE_PAD), BF16).at[:, :main + GLA_GATE_RANK].set(
        w_in.astype(BF16))
    wg_pad = jnp.zeros((GLA_GATE_PAD, GLA_KEY_WIDTH), BF16).at[:GLA_GATE_RANK].set(
        w_gate_up.astype(BF16))
    outs = [_row_out(tokens, GLA_KEY_WIDTH, TM_PROJ), _row_out(tokens, GLA_KEY_WIDTH, TM_PROJ),
            _row_out(tokens, D_MODEL, TM_PROJ), _row_out(tokens, D_MODEL, TM_PROJ),
            _row_out(tokens, GLA_KEY_WIDTH, TM_PROJ, F32)]
    specs, shapes = zip(*outs)
    return pl.pallas_call(
        _inproj_c_kernel,
        grid=grid,
        in_specs=[x_spec, mod_spec, _resident((1, D_MODEL)),
                  _resident((D_MODEL, main + GLA_GATE_PAD)),
                  _resident((GLA_GATE_PAD, GLA_KEY_WIDTH)), _resident((1, GLA_KEY_WIDTH))],
        out_specs=list(specs), out_shape=list(shapes),
        compiler_params=_params("parallel"),
        name="inproj_gla",
    )(x2, mod, gain.reshape(1, D_MODEL), w_pad, wg_pad, b_gate.reshape(1, GLA_KEY_WIDTH))


def _attn_a_kernel(q_ref, kp_ref, kc_ref, vp_ref, vc_ref, bias_ref, o_ref, kwin, vwin):
    first_block = pl.program_id(2) == 0
    kwin[0:A_GROUP, :] = kp_ref[...]
    kwin[A_GROUP:2 * A_GROUP, :] = kc_ref[...]
    vwin[0:A_GROUP, :] = vp_ref[...]
    vwin[A_GROUP:2 * A_GROUP, :] = vc_ref[...]
    tail = jnp.zeros((A_BAND_PAD - A_BAND, HEAD_PAIR), BF16)
    kwin[2 * A_GROUP:, :] = tail
    vwin[2 * A_GROUP:, :] = tail

    lane = lax.broadcasted_iota(jnp.int32, (CHUNK, HEAD_PAIR), 1)
    col = lax.broadcasted_iota(jnp.int32, (CHUNK, A_BAND_PAD), 1)
    low_half = lane < A_HEAD_DIM
    for c in range(A_GROUP // CHUNK):
        qc = q_ref[c * CHUNK:(c + 1) * CHUNK, :]
        kb = kwin[c * CHUNK:c * CHUNK + A_BAND_PAD, :]
        vb = vwin[c * CHUNK:c * CHUNK + A_BAND_PAD, :]
        valid = col >= jnp.where(first_block, A_GROUP - c * CHUNK, 0)
        outs = []
        for half in range(2):
            qm = jnp.where(low_half if half == 0 else jnp.logical_not(low_half), qc, 0)
            s = _dot_nt(qm, kb) + bias_ref[half]
            s = jnp.where(valid, s, -jnp.inf)
            p = jnp.exp(s - jnp.max(s, axis=-1, keepdims=True))
            denom = jnp.sum(p, axis=-1, keepdims=True)
            outs.append(_dot(p.astype(BF16), vb) / denom)
        o_ref[c * CHUNK:(c + 1) * CHUNK, :] = jnp.where(low_half, outs[0], outs[1]).astype(BF16)


def _relpos_bias_table(rel_bias):
    pad = A_LEFT_CHUNKS * CHUNK
    rel = np.arange(CHUNK)[:, None] + pad - np.arange(A_BAND)[None, :]
    idx = np.clip(rel, -(CHUNK - 1), A_MAX_REL) + (CHUNK - 1)
    table = rel_bias.astype(F32)[:, idx]
    return jnp.pad(table, ((0, 0), (0, 0), (0, A_BAND_PAD - A_BAND)), constant_values=-jnp.inf)


def _attn_a(q, k, v, rel_bias, batch, seq):
    tokens = q.shape[0]
    blocks = seq // A_GROUP
    pairs = D_MODEL // HEAD_PAIR
    cur = lambda b, h, j: (b * blocks + j, h)
    prev = lambda b, h, j: (b * blocks + jnp.maximum(j - 1, 0), h)
    blk = (A_GROUP, HEAD_PAIR)
    win_rows = 2 * A_GROUP + A_BAND_PAD - A_BAND
    return pl.pallas_call(
        _attn_a_kernel,
        grid=(batch, pairs, blocks),
        in_specs=[pl.BlockSpec(blk, cur), pl.BlockSpec(blk, prev), pl.BlockSpec(blk, cur),
                  pl.BlockSpec(blk, prev), pl.BlockSpec(blk, cur),
                  pl.BlockSpec((2, CHUNK, A_BAND_PAD), lambda b, h, j: (h, 0, 0))],
        out_specs=pl.BlockSpec(blk, cur),
        out_shape=jax.ShapeDtypeStruct((tokens, D_MODEL), BF16),
        scratch_shapes=[pltpu.VMEM((win_rows, HEAD_PAIR), BF16),
                        pltpu.VMEM((win_rows, HEAD_PAIR), BF16)],
        compiler_params=_params("parallel", "parallel", "arbitrary"),
        name="relpos_band_attention",
    )(q, k, k, v, v, _relpos_bias_table(rel_bias))


def _mixb_kernel(x_ref, mod_ref, gain_ref, w_ref, vg_ref, ws_ref, bs_ref, o_ref, v_scr):
    h = _normed_input(x_ref, mod_ref, gain_ref)
    tm = h.shape[0]
    slab = 512
    ssq = jnp.zeros((tm, 1), F32)
    for c in range(SG_HALF // slab):
        zc = _gelu(_dot(h, w_ref[:, SG_HALF + c * slab:SG_HALF + (c + 1) * slab]))
        ssq = ssq + jnp.sum(zc * zc, axis=-1, keepdims=True)
        v_scr[:, c * slab:(c + 1) * slab] = zc
    inv_rms = lax.rsqrt(ssq * (1.0 / SG_HALF) + RMS_EPS)

    row = lax.broadcasted_iota(jnp.int32, (SG_WINDOW, SG_WINDOW), 0) // CHUNK
    colc = lax.broadcasted_iota(jnp.int32, (SG_WINDOW, SG_WINDOW), 1) // CHUNK
    block_lower = row >= colc
    gw = SG_GROUP_WIDTH
    for g in range(SG_GROUPS):
        cols = slice(g * gw, (g + 1) * gw)
        w_s = jnp.where(block_lower, ws_ref[g], 0.0).astype(BF16)
        vn = (v_scr[:, cols] * inv_rms * vg_ref[:, cols]).astype(BF16)
        u = _gelu(_dot(h, w_ref[:, g * gw:(g + 1) * gw]))
        for w in range(tm // SG_WINDOW):
            rows = slice(w * SG_WINDOW, (w + 1) * SG_WINDOW)
            vm = _dot(w_s, vn[rows]) + bs_ref[g]
            o_ref[rows, cols] = (u[rows] * vm).astype(BF16)


def _mixb(x2, mod, gain, w_in, v_gain, w_s, b_s, seq):
    tokens = x2.shape[0]
    grid, x_spec, mod_spec = _row_specs(tokens, seq, TM_MIXB)
    bias = jnp.broadcast_to(b_s.astype(F32)[:, :, None], (SG_GROUPS, SG_WINDOW, SG_GROUP_WIDTH))
    out_spec, out_shape = _row_out(tokens, SG_HALF, TM_MIXB)
    return pl.pallas_call(
        _mixb_kernel,
        grid=grid,
        in_specs=[x_spec, mod_spec, _resident((1, D_MODEL)), _resident((D_MODEL, SG_FF)),
                  _resident((1, SG_HALF)), _resident((SG_GROUPS, SG_WINDOW, SG_WINDOW)),
                  _resident((SG_GROUPS, SG_WINDOW, SG_GROUP_WIDTH))],
        out_specs=out_spec, out_shape=out_shape,
        scratch_shapes=[pltpu.VMEM((TM_MIXB, SG_HALF), F32)],
        compiler_params=_params("parallel"),
        name="gmlp_spatial_gating",
    )(x2, mod, gain.reshape(1, D_MODEL), w_in.astype(BF16), v_gain.reshape(1, SG_HALF),
      w_s.astype(F32), bias)


def _gla_decay_matrix():
    r = np.arange(CHUNK)[:, None]
    i = np.arange(CHUNK)[None, :]
    parts = [i <= r, i > r]
    for level in range(GLA_LEVELS):
        n = (CHUNK // 2) >> level
        ref = (r // (2 * n)) * 2 * n + n
        upper = (r % (2 * n)) >= n
        parts.append(np.where(upper, (i > ref) & (i <= r), (i > r) & (i <= ref)))
    parts += [np.zeros((CHUNK, CHUNK), bool)] * (8 - len(parts))
    return jnp.asarray(np.concatenate(parts, axis=0), BF16)


def _gla_level_map():
    t = np.arange(CHUNK)[:, None]
    s = np.arange(CHUNK)[None, :]
    out = np.full((CHUNK, CHUNK), -1, np.int32)
    out[t == s] = 0
    for level in range(GLA_LEVELS):
        n = (CHUNK // 2) >> level
        hit = (t // (2 * n) == s // (2 * n)) & (t % (2 * n) >= n) & (s % (2 * n) < n)
        out[hit] = 1 + level
    return jnp.asarray(out)


def _gla_kernel(q_ref, k_ref, v_ref, r_ref, la_ref, dm_ref, lvl_ref, og_ref, o_ref, state):
    @pl.when(pl.program_id(1) == 0)
    def _():
        state[...] = jnp.zeros_like(state)

    dm = dm_ref[...]
    lvl = lvl_ref[...]

    def chunk(g, carry):
        rows = pl.ds(pl.multiple_of(g * CHUNK, CHUNK), CHUNK)
        la = la_ref[rows, :]
        la_hi = la.astype(BF16)
        la_lo = (la - la_hi.astype(F32)).astype(BF16)
        decay = jnp.exp(_dot(dm, la_hi) + _dot(dm, la_lo))
        for h in range(GLA_HEADS):
            kc = slice(h * GLA_DK, (h + 1) * GLA_DK)
            vc = slice(h * GLA_DV, (h + 1) * GLA_DV)
            qh = q_ref[rows, kc].astype(F32)
            kh = k_ref[rows, kc].astype(F32)
            vh = v_ref[rows, vc]
            s_t = state[h]
            o = _dot_nt((qh * decay[0:CHUNK, kc]).astype(BF16), s_t.astype(BF16))
            att = jnp.where(lvl == 0, _dot_nt(qh.astype(BF16), kh.astype(BF16)), 0.0)
            for level in range(GLA_LEVELS):
                e = decay[(2 + level) * CHUNK:(3 + level) * CHUNK, kc]
                pair = _dot_nt((qh * e).astype(BF16), (kh * e).astype(BF16))
                att = jnp.where(lvl == 1 + level, pair, att)
            o = o + _dot(att.astype(BF16), vh)
            k_dec = (kh * decay[CHUNK:2 * CHUNK, kc]).astype(BF16)
            state[h] = (s_t * decay[CHUNK - 1:CHUNK, kc]
                        + lax.dot_general(vh, k_dec, _TN, preferred_element_type=F32))
            ms = jnp.mean(o * o, axis=-1, keepdims=True)
            on = o * lax.rsqrt(ms + RMS_EPS) * og_ref[:, vc]
            o_ref[rows, vc] = (jax.nn.silu(r_ref[rows, vc].astype(F32)) * on).astype(BF16)
        return carry

    lax.fori_loop(0, GLA_STEP_CHUNKS, chunk, 0)


def _gla(q, k, v, r, la, o_gain, batch, seq):
    tokens = q.shape[0]
    step = GLA_STEP_CHUNKS * CHUNK
    blocks = seq // step
    idx = lambda b, j: (b * blocks + j, 0)
    return pl.pallas_call(
        _gla_kernel,
        grid=(batch, blocks),
        in_specs=[pl.BlockSpec((step, GLA_KEY_WIDTH), idx), pl.BlockSpec((step, GLA_KEY_WIDTH), idx),
                  pl.BlockSpec((step, D_MODEL), idx), pl.BlockSpec((step, D_MODEL), idx),
                  pl.BlockSpec((step, GLA_KEY_WIDTH), idx),
                  _resident((8 * CHUNK, CHUNK)), _resident((CHUNK, CHUNK)), _resident((1, D_MODEL))],
        out_specs=pl.BlockSpec((step, D_MODEL), idx),
        out_shape=jax.ShapeDtypeStruct((tokens, D_MODEL), BF16),
        scratch_shapes=[pltpu.VMEM((GLA_HEADS, GLA_DV, GLA_DK), F32)],
        compiler_params=_params("parallel", "arbitrary"),
        name="gla_scan",
    )(q, k, v, r, la, _gla_decay_matrix(), _gla_level_map(),
      jnp.tile(o_gain, GLA_HEADS).reshape(1, D_MODEL))


def _sb_kernel(q_ref, k_ref, v_ref, tri_ref, o_ref):
    qb = pl.program_id(2)
    tri = tri_ref[...]
    lane = lax.broadcasted_iota(jnp.int32, (SB_BLOCK, HEAD_PAIR), 1)
    low_half = lane < SB_HEAD_DIM
    q = q_ref[...]
    q_heads = (jnp.where(low_half, q, 0), jnp.where(low_half, 0, q))
    t_idx = lax.broadcasted_iota(jnp.int32, (SB_BLOCK, SB_BLOCK), 0)
    s_idx = lax.broadcasted_iota(jnp.int32, (SB_BLOCK, SB_BLOCK), 1)
    strict = s_idx < t_idx

    def visit(qh, kblk, vblk, later, acc, masked):
        z = _dot_nt(qh, kblk)
        sp = jnp.maximum(z, 0.0) + jnp.log(1.0 + jnp.exp(-jnp.abs(z)))
        if masked:
            sp = jnp.where(strict, sp, 0.0)
        hi = sp.astype(BF16)
        lo = (sp - hi.astype(F32)).astype(BF16)
        incl = _dot(hi, tri) + _dot(lo, tri)
        w = jnp.exp(z - incl - later)
        if masked:
            w = jnp.where(strict, w, 0.0)
        return later + incl[:, 0:1], acc + _dot(w.astype(BF16), vblk)

    def block_rows(kb):
        return pl.ds(pl.multiple_of(kb * SB_BLOCK, SB_BLOCK), SB_BLOCK)

    diag = block_rows(qb)
    kblk, vblk = k_ref[diag, :], v_ref[diag, :]
    init = []
    for qh in q_heads:
        init.extend(visit(qh, kblk, vblk, jnp.zeros((SB_BLOCK, 1), F32),
                          jnp.zeros((SB_BLOCK, HEAD_PAIR), F32), True))

    def body(i, carry):
        rows = block_rows(qb - 1 - i)
        kblk, vblk = k_ref[rows, :], v_ref[rows, :]
        out = []
        for half, qh in enumerate(q_heads):
            out.extend(visit(qh, kblk, vblk, carry[2 * half], carry[2 * half + 1], False))
        return tuple(out)

    final = lax.fori_loop(0, qb, body, tuple(init))
    o_ref[...] = jnp.where(low_half, final[1], final[3]).astype(BF16)


def _stick_breaking(q, k, v, batch, seq):
    tokens = q.shape[0]
    blocks = seq // SB_BLOCK
    pairs = D_MODEL // HEAD_PAIR
    j = np.arange(SB_BLOCK)
    tri = jnp.asarray(j[:, None] >= j[None, :], BF16)
    qo_spec = pl.BlockSpec((SB_BLOCK, HEAD_PAIR), lambda b, h, i: (b * blocks + i, h))
    kv_spec = pl.BlockSpec((seq, HEAD_PAIR), lambda b, h, i: (b, h))
    return pl.pallas_call(
        _sb_kernel,
        grid=(batch, pairs, blocks),
        in_specs=[qo_spec, kv_spec, kv_spec, _resident((SB_BLOCK, SB_BLOCK))],
        out_specs=qo_spec,
        out_shape=jax.ShapeDtypeStruct((tokens, D_MODEL), BF16),
        compiler_params=_params("parallel", "parallel", "arbitrary"),
        name="stick_breaking_attention",
    )(q, k, v, tri)


def _post_kernel(o_ref, x_ref, mod_ref, gain_ref, wo_ref, w1_ref, w2_ref, out_ref):
    gate1 = mod_ref[0, 2:3, :]
    shift2, scale2, gate2 = mod_ref[0, 3:4, :], mod_ref[0, 4:5, :], mod_ref[0, 5:6, :]
    x1 = x_ref[...] + gate1 * _dot(o_ref[...], wo_ref[...])
    h = _modulated_norm(x1, gain_ref[...], scale2, shift2).astype(BF16)
    acc = jnp.zeros_like(x1)
    for c in range(FFN_HIDDEN // FFN_SLAB):
        t = _dot(h, w1_ref[:, c * FFN_SLAB:(c + 1) * FFN_SLAB])
        t = jnp.square(jnp.maximum(t, 0.0)).astype(BF16)
        acc = acc + _dot(t, w2_ref[c * FFN_SLAB:(c + 1) * FFN_SLAB, :])
    out_ref[...] = x1 + gate2 * acc


def _post(o, x2, mod, gain, w_out, w1, w2, seq):
    tokens = x2.shape[0]
    grid, x_spec, mod_spec = _row_specs(tokens, seq, TM_PROJ)
    width = o.shape[1]
    out_spec, out_shape = _row_out(tokens, D_MODEL, TM_PROJ, F32)
    return pl.pallas_call(
        _post_kernel,
        grid=grid,
        in_specs=[pl.BlockSpec((TM_PROJ, width), lambda i: (i, 0)), x_spec, mod_spec,
                  _resident((1, D_MODEL)), _resident((width, D_MODEL)),
                  _resident((D_MODEL, FFN_HIDDEN)), _resident((FFN_HIDDEN, D_MODEL))],
        out_specs=out_spec, out_shape=out_shape,
        compiler_params=_params("parallel"),
        name="outproj_mlp",
    )(o, x2, mod, gain.reshape(1, D_MODEL), w_out.astype(BF16), w1.astype(BF16), w2.astype(BF16))


def kernel(x, c, ada_w, ada_b, norm_mix, norm_ffn, ffn_w1, ffn_w2, a_w_in, a_q_gain, a_k_gain, a_rel_bias, a_w_out, b_w_in, b_v_gain, b_w_s, b_b_s, b_w_out, c_w_in, c_w_gate_up, c_b_gate, c_o_gain, c_w_out, d_w_in, d_w_out):
    batch, seq, _ = x.shape
    x2 = x.reshape(batch * seq, D_MODEL)
    mods = _modulation(c, ada_w, ada_b)
    for i in range(DEPTH):
        m, j = i % N_MIXERS, i // N_MIXERS
        mod = mods[i]
        if m == 0:
            q, k, v = _inproj_a(x2, mod, norm_mix[i], a_w_in[j], a_q_gain[j], a_k_gain[j], seq)
            o = _attn_a(q, k, v, a_rel_bias[j], batch, seq)
            w_out = a_w_out[j]
        elif m == 1:
            o = _mixb(x2, mod, norm_mix[i], b_w_in[j], b_v_gain[j], b_w_s[j], b_b_s[j], seq)
            w_out = b_w_out[j]
        elif m == 2:
            q, k, v, r, la = _inproj_c(x2, mod, norm_mix[i], c_w_in[j], c_w_gate_up[j], c_b_gate[j], seq)
            o = _gla(q, k, v, r, la, c_o_gain[j], batch, seq)
            w_out = c_w_out[j]
        else:
            q, k, v = _inproj_d(x2, mod, norm_mix[i], d_w_in[j], seq)
            o = _stick_breaking(q, k, v, batch, seq)
            w_out = d_w_out[j]
        x2 = _post(o, x2, mod, norm_ffn[i], w_out, ffn_w1[i], ffn_w2[i], seq)
    return x2.reshape(batch, seq, D_MODEL)
```

```python
import functools

import jax
import jax.numpy as jnp
import numpy as np
from jax import lax
from jax.experimental import pallas as pl
from jax.experimental.pallas import tpu as pltpu

F32 = jnp.float32
BF16 = jnp.bfloat16

D_MODEL = 1024
DEPTH = 4
N_MIXERS = 4
CHUNK = 64
RMS_EPS = 1e-6

A_HEADS = 16
A_HEAD_DIM = 64
A_LEFT_CHUNKS = 8
A_MAX_REL = 256
A_BAND = (A_LEFT_CHUNKS + 1) * CHUNK
A_BAND_PAD = 640
A_GROUP = 8 * CHUNK

SG_FF = 6 * D_MODEL
SG_HALF = SG_FF // 2
SG_GROUPS = 8
SG_WINDOW = 128
SG_GROUP_WIDTH = SG_HALF // SG_GROUPS

GLA_HEADS = 4
GLA_KEY_WIDTH = D_MODEL // 2
GLA_DK = 128
GLA_DV = 256
GLA_GATE_RANK = 16
GLA_GATE_PAD = 128
GLA_TAU = 16.0
GLA_LEVELS = 6
GLA_STEP_CHUNKS = 8

SB_HEADS = 16
SB_HEAD_DIM = 64
SB_QROWS = 512
SB_KBLOCK = 256
SB_STRIP = 64
LOG2_E = float(np.log2(np.e))
SB_Q_SCALE = SB_HEAD_DIM ** -0.5 * LOG2_E
A_Q_SCALE = A_HEAD_DIM ** -0.5 * LOG2_E

FFN_HIDDEN = 4 * D_MODEL
FFN_SLAB = 1024

LANES = 128
HEAD_PAIR = LANES
VMEM_LIMIT_BYTES = 56 * 1024 * 1024

TM_PROJ = 512
TM_MIXB = 256

_NT = (((1,), (1,)), ((), ()))
_TN = (((0,), (0,)), ((), ()))


def _params(*semantics):
    return pltpu.CompilerParams(dimension_semantics=semantics,
                                vmem_limit_bytes=VMEM_LIMIT_BYTES)


def _dot(a, b):
    return jnp.dot(a, b, preferred_element_type=F32)


def _dot_nt(a, b):
    return lax.dot_general(a, b, _NT, preferred_element_type=F32)


def _resident(shape):
    zeros = (0,) * len(shape)
    return pl.BlockSpec(shape, lambda *_: zeros, pipeline_mode=pl.Buffered(1))


def _gelu(x):
    return 0.5 * x * (1.0 + lax.erf(x * (2.0 ** -0.5)))


def _modulated_norm(x, gain, scale, shift):
    ms = jnp.mean(x * x, axis=-1, keepdims=True)
    y = x * lax.rsqrt(ms + RMS_EPS) * gain
    return y * (1.0 + scale) + shift


def _mod_kernel(c_ref, w_ref, b_ref, o_ref):
    cond = jax.nn.silu(c_ref[...])
    o_ref[0] = jnp.dot(cond, w_ref[0], precision=lax.Precision.HIGHEST,
                       preferred_element_type=F32) + b_ref[0]


def _modulation(c, ada_w, ada_b):
    batch = c.shape[0]
    rows = 8
    c_pad = jnp.zeros((rows, D_MODEL), F32).at[:batch].set(c)
    tn = 1536
    out = pl.pallas_call(
        _mod_kernel,
        grid=(DEPTH, 6 * D_MODEL // tn),
        in_specs=[
            pl.BlockSpec((rows, D_MODEL), lambda i, j: (0, 0)),
            pl.BlockSpec((1, D_MODEL, tn), lambda i, j: (i, 0, j)),
            pl.BlockSpec((1, 1, tn), lambda i, j: (i, 0, j)),
        ],
        out_specs=pl.BlockSpec((1, rows, tn), lambda i, j: (i, 0, j)),
        out_shape=jax.ShapeDtypeStruct((DEPTH, rows, 6 * D_MODEL), F32),
        compiler_params=_params("parallel", "parallel"),
        name="adaln_modulation",
    )(c_pad, ada_w, ada_b.reshape(DEPTH, 1, 6 * D_MODEL))
    return out[:, :batch].reshape(DEPTH, batch, 6, D_MODEL)


def _normed_input(x_ref, mod_ref, gain_ref):
    shift, scale = mod_ref[0, 0:1, :], mod_ref[0, 1:2, :]
    return _modulated_norm(x_ref[...], gain_ref[...], scale, shift).astype(BF16)


def _inproj_a_kernel(x_ref, mod_ref, gain_ref, w_ref, qg_ref, kg_ref, seg_ref,
                     q_ref, k_ref, v_ref):
    h = _normed_input(x_ref, mod_ref, gain_ref)
    seg = seg_ref[...]

    def head_norm(y, gain):
        ssq = _dot((y * y).astype(BF16), seg)
        return y * lax.rsqrt(ssq * (1.0 / A_HEAD_DIM) + RMS_EPS) * gain

    width = seg.shape[0]
    for c in range(D_MODEL // width):
        cols = slice(c * width, (c + 1) * width)
        yq = _dot(h, w_ref[:, c * width:(c + 1) * width])
        q_ref[:, cols] = (head_norm(yq, qg_ref[:, cols]) * A_Q_SCALE).astype(BF16)
        yk = _dot(h, w_ref[:, D_MODEL + c * width:D_MODEL + (c + 1) * width])
        k_ref[:, cols] = head_norm(yk, kg_ref[:, cols]).astype(BF16)
        yv = _dot(h, w_ref[:, 2 * D_MODEL + c * width:2 * D_MODEL + (c + 1) * width])
        v_ref[:, cols] = yv.astype(BF16)


def _inproj_d_kernel(x_ref, mod_ref, gain_ref, w_ref, q_ref, k_ref, v_ref):
    h = _normed_input(x_ref, mod_ref, gain_ref)
    width = 512
    for c in range(D_MODEL // width):
        cols = slice(c * width, (c + 1) * width)
        yq = _dot(h, w_ref[:, c * width:(c + 1) * width])
        q_ref[:, cols] = (yq * SB_Q_SCALE).astype(BF16)
        yk = _dot(h, w_ref[:, D_MODEL + c * width:D_MODEL + (c + 1) * width])
        k_ref[:, cols] = yk.astype(BF16)
        yv = _dot(h, w_ref[:, 2 * D_MODEL + c * width:2 * D_MODEL + (c + 1) * width])
        v_ref[:, cols] = yv.astype(BF16)


def _inproj_c_kernel(x_ref, mod_ref, gain_ref, w_ref, wg_ref, bg_ref,
                     q_ref, k_ref, v_ref, r_ref, la_ref):
    h = _normed_input(x_ref, mod_ref, gain_ref)
    kw = GLA_KEY_WIDTH
    q_ref[...] = (_dot(h, w_ref[:, 0:kw]) * (GLA_DK ** -0.5)).astype(BF16)
    k_ref[...] = _dot(h, w_ref[:, kw:2 * kw]).astype(BF16)
    for c in range(2):
        cols = slice(c * kw, (c + 1) * kw)
        v_ref[:, cols] = _dot(h, w_ref[:, 2 * kw + c * kw:2 * kw + (c + 1) * kw]).astype(BF16)
        r_ref[:, cols] = _dot(h, w_ref[:, 4 * kw + c * kw:4 * kw + (c + 1) * kw]).astype(BF16)
    a = _dot(h, w_ref[:, 6 * kw:6 * kw + GLA_GATE_PAD]).astype(BF16)
    gate = _dot(a, wg_ref[...]) + bg_ref[...]
    la_ref[...] = jax.nn.log_sigmoid(gate) * (1.0 / GLA_TAU)


def _row_specs(tokens, seq, tm):
    tiles_per_batch = seq // tm
    x_spec = pl.BlockSpec((tm, D_MODEL), lambda i: (i, 0))
    mod_spec = pl.BlockSpec((1, 6, D_MODEL), lambda i: (i // tiles_per_batch, 0, 0))
    return (tokens // tm,), x_spec, mod_spec


def _row_out(tokens, width, tm, dtype=BF16):
    return (pl.BlockSpec((tm, width), lambda i: (i, 0)),
            jax.ShapeDtypeStruct((tokens, width), dtype))


def _inproj_a(x2, mod, gain, w_in, q_gain, k_gain, seq):
    tokens = x2.shape[0]
    grid, x_spec, mod_spec = _row_specs(tokens, seq, TM_PROJ)
    seg_width = 256
    seg_id = np.arange(seg_width) // A_HEAD_DIM
    seg = jnp.asarray(seg_id[:, None] == seg_id[None, :], BF16)
    specs, shapes = zip(*[_row_out(tokens, D_MODEL, TM_PROJ)] * 3)
    return pl.pallas_call(
        _inproj_a_kernel,
        grid=grid,
        in_specs=[x_spec, mod_spec, _resident((1, D_MODEL)), _resident((D_MODEL, 3 * D_MODEL)),
                  _resident((1, D_MODEL)), _resident((1, D_MODEL)),
                  _resident((seg_width, seg_width))],
        out_specs=list(specs), out_shape=list(shapes),
        compiler_params=_params("parallel"),
        name="inproj_relpos_attention",
    )(x2, mod, gain.reshape(1, D_MODEL), w_in.astype(BF16),
      jnp.tile(q_gain, A_HEADS).reshape(1, D_MODEL), jnp.tile(k_gain, A_HEADS).reshape(1, D_MODEL), seg)


def _inproj_d(x2, mod, gain, w_in, seq):
    tokens = x2.shape[0]
    grid, x_spec, mod_spec = _row_specs(tokens, seq, TM_PROJ)
    specs, shapes = zip(*[_row_out(tokens, D_MODEL, TM_PROJ)] * 3)
    return pl.pallas_call(
        _inproj_d_kernel,
        grid=grid,
        in_specs=[x_spec, mod_spec, _resident((1, D_MODEL)), _resident((D_MODEL, 3 * D_MODEL))],
        out_specs=list(specs), out_shape=list(shapes),
        compiler_params=_params("parallel"),
        name="inproj_stick_breaking",
    )(x2, mod, gain.reshape(1, D_MODEL), w_in.astype(BF16))


def _inproj_c(x2, mod, gain, w_in, w_gate_up, b_gate, seq):
    tokens = x2.shape[0]
    grid, x_spec, mod_spec = _row_specs(tokens, seq, TM_PROJ)
    main = 2 * GLA_KEY_WIDTH + 2 * D_MODEL
    w_pad = jnp.zeros((D_MODEL, main + GLA_GATE_PAD), BF16).at[:, :main + GLA_GATE_RANK].set(
        w_in.astype(BF16))
    wg_pad = jnp.zeros((GLA_GATE_PAD, GLA_KEY_WIDTH), BF16).at[:GLA_GATE_RANK].set(
        w_gate_up.astype(BF16))
    outs = [_row_out(tokens, GLA_KEY_WIDTH, TM_PROJ), _row_out(tokens, GLA_KEY_WIDTH, TM_PROJ),
            _row_out(tokens, D_MODEL, TM_PROJ), _row_out(tokens, D_MODEL, TM_PROJ),
            _row_out(tokens, GLA_KEY_WIDTH, TM_PROJ, F32)]
    specs, shapes = zip(*outs)
    return pl.pallas_call(
        _inproj_c_kernel,
        grid=grid,
        in_specs=[x_spec, mod_spec, _resident((1, D_MODEL)),
                  _resident((D_MODEL, main + GLA_GATE_PAD)),
                  _resident((GLA_GATE_PAD, GLA_KEY_WIDTH)), _resident((1, GLA_KEY_WIDTH))],
        out_specs=list(specs), out_shape=list(shapes),
        compiler_params=_params("parallel"),
        name="inproj_gla",
    )(x2, mod, gain.reshape(1, D_MODEL), w_pad, wg_pad, b_gate.reshape(1, GLA_KEY_WIDTH))


def _attn_a_kernel(q_ref, kp_ref, kc_ref, vp_ref, vc_ref, bias_ref, o_ref, kwin, vwin, bias_scr):
    block = pl.program_id(2)
    kwin[0:A_GROUP, :] = kp_ref[...]
    kwin[A_GROUP:2 * A_GROUP, :] = kc_ref[...]
    vwin[0:A_GROUP, :] = vp_ref[...]
    vwin[A_GROUP:2 * A_GROUP, :] = vc_ref[...]
    tail = jnp.zeros((A_BAND_PAD - A_BAND, HEAD_PAIR), BF16)
    kwin[2 * A_GROUP:, :] = tail
    vwin[2 * A_GROUP:, :] = tail

    lane = lax.broadcasted_iota(jnp.int32, (CHUNK, HEAD_PAIR), 1)
    low_half = lane < A_HEAD_DIM
    chunks = A_GROUP // CHUNK

    @pl.when(block == 0)
    def _():
        col = lax.broadcasted_iota(jnp.int32, (2 * CHUNK, A_BAND_PAD), 1)
        for c in range(chunks):
            bias_scr[c] = jnp.where(col >= A_GROUP - c * CHUNK, bias_ref[0], -jnp.inf)

    @pl.when(block == 1)
    def _():
        for c in range(chunks):
            bias_scr[c] = bias_ref[0]

    scores = []
    for c in range(chunks):
        qc = q_ref[c * CHUNK:(c + 1) * CHUNK, :]
        q2 = jnp.concatenate([jnp.where(low_half, qc, 0), jnp.where(low_half, 0, qc)], axis=0)
        scores.append(_dot_nt(q2, kwin[c * CHUNK:c * CHUNK + A_BAND_PAD, :]) + bias_scr[c])
    probs = [jnp.exp2(s - jnp.max(s, axis=-1, keepdims=True)) for s in scores]
    denoms = [jnp.sum(p, axis=-1, keepdims=True) for p in probs]
    outs = [_dot(p.astype(BF16), vwin[c * CHUNK:c * CHUNK + A_BAND_PAD, :]) / d
            for c, (p, d) in enumerate(zip(probs, denoms))]
    for c in range(chunks):
        o_ref[c * CHUNK:(c + 1) * CHUNK, :] = jnp.where(
            low_half, outs[c][0:CHUNK], outs[c][CHUNK:2 * CHUNK]).astype(BF16)


def _relpos_bias_table(rel_bias):
    pad = A_LEFT_CHUNKS * CHUNK
    rel = np.arange(CHUNK)[:, None] + pad - np.arange(A_BAND)[None, :]
    idx = np.clip(rel, -(CHUNK - 1), A_MAX_REL) + (CHUNK - 1)
    table = rel_bias.astype(F32)[:, idx] * LOG2_E
    table = jnp.pad(table, ((0, 0), (0, 0), (0, A_BAND_PAD - A_BAND)), constant_values=-jnp.inf)
    return table.reshape(A_HEADS // 2, 2 * CHUNK, A_BAND_PAD)


def _attn_a(q, k, v, rel_bias, batch, seq):
    tokens = q.shape[0]
    blocks = seq // A_GROUP
    pairs = D_MODEL // HEAD_PAIR
    cur = lambda b, h, j: (b * blocks + j, h)
    prev = lambda b, h, j: (b * blocks + jnp.maximum(j - 1, 0), h)
    blk = (A_GROUP, HEAD_PAIR)
    win_rows = 2 * A_GROUP + A_BAND_PAD - A_BAND
    return pl.pallas_call(
        _attn_a_kernel,
        grid=(batch, pairs, blocks),
        in_specs=[pl.BlockSpec(blk, cur), pl.BlockSpec(blk, prev), pl.BlockSpec(blk, cur),
                  pl.BlockSpec(blk, prev), pl.BlockSpec(blk, cur),
                  pl.BlockSpec((1, 2 * CHUNK, A_BAND_PAD), lambda b, h, j: (h, 0, 0))],
        out_specs=pl.BlockSpec(blk, cur),
        out_shape=jax.ShapeDtypeStruct((tokens, D_MODEL), BF16),
        scratch_shapes=[pltpu.VMEM((win_rows, HEAD_PAIR), BF16),
                        pltpu.VMEM((win_rows, HEAD_PAIR), BF16),
                        pltpu.VMEM((A_GROUP // CHUNK, 2 * CHUNK, A_BAND_PAD), F32)],
        compiler_params=_params("parallel", "parallel", "arbitrary"),
        name="relpos_band_attention",
    )(q, k, k, v, v, _relpos_bias_table(rel_bias))


def _mixb_kernel(x_ref, mod_ref, gain_ref, w_ref, vg_ref, ws_ref, bs_ref, o_ref, v_scr):
    h = _normed_input(x_ref, mod_ref, gain_ref)
    tm = h.shape[0]
    slab = 512
    ssq = jnp.zeros((tm, 1), F32)
    for c in range(SG_HALF // slab):
        zc = _gelu(_dot(h, w_ref[:, SG_HALF + c * slab:SG_HALF + (c + 1) * slab]))
        ssq = ssq + jnp.sum(zc * zc, axis=-1, keepdims=True)
        v_scr[:, c * slab:(c + 1) * slab] = zc
    inv_rms = lax.rsqrt(ssq * (1.0 / SG_HALF) + RMS_EPS)

    row = lax.broadcasted_iota(jnp.int32, (SG_WINDOW, SG_WINDOW), 0) // CHUNK
    colc = lax.broadcasted_iota(jnp.int32, (SG_WINDOW, SG_WINDOW), 1) // CHUNK
    block_lower = row >= colc
    gw = SG_GROUP_WIDTH
    for g in range(SG_GROUPS):
        cols = slice(g * gw, (g + 1) * gw)
        w_s = jnp.where(block_lower, ws_ref[g], 0.0).astype(BF16)
        vn = (v_scr[:, cols] * inv_rms * vg_ref[:, cols]).astype(BF16)
        u = _gelu(_dot(h, w_ref[:, g * gw:(g + 1) * gw]))
        for w in range(tm // SG_WINDOW):
            rows = slice(w * SG_WINDOW, (w + 1) * SG_WINDOW)
            vm = _dot(w_s, vn[rows]) + bs_ref[g]
            o_ref[rows, cols] = (u[rows] * vm).astype(BF16)


def _mixb(x2, mod, gain, w_in, v_gain, w_s, b_s, seq):
    tokens = x2.shape[0]
    grid, x_spec, mod_spec = _row_specs(tokens, seq, TM_MIXB)
    bias = jnp.broadcast_to(b_s.astype(F32)[:, :, None], (SG_GROUPS, SG_WINDOW, SG_GROUP_WIDTH))
    out_spec, out_shape = _row_out(tokens, SG_HALF, TM_MIXB)
    return pl.pallas_call(
        _mixb_kernel,
        grid=grid,
        in_specs=[x_spec, mod_spec, _resident((1, D_MODEL)), _resident((D_MODEL, SG_FF)),
                  _resident((1, SG_HALF)), _resident((SG_GROUPS, SG_WINDOW, SG_WINDOW)),
                  _resident((SG_GROUPS, SG_WINDOW, SG_GROUP_WIDTH))],
        out_specs=out_spec, out_shape=out_shape,
        scratch_shapes=[pltpu.VMEM((TM_MIXB, SG_HALF), F32)],
        compiler_params=_params("parallel"),
        name="gmlp_spatial_gating",
    )(x2, mod, gain.reshape(1, D_MODEL), w_in.astype(BF16), v_gain.reshape(1, SG_HALF),
      w_s.astype(F32), bias)


def _gla_decay_matrix():
    r = np.arange(CHUNK)[:, None]
    i = np.arange(CHUNK)[None, :]
    parts = [i <= r, i > r]
    for level in range(GLA_LEVELS):
        n = (CHUNK // 2) >> level
        ref = (r // (2 * n)) * 2 * n + n
        upper = (r % (2 * n)) >= n
        parts.append(np.where(upper, (i > ref) & (i <= r), (i > r) & (i <= ref)))
    parts += [np.zeros((CHUNK, CHUNK), bool)] * (8 - len(parts))
    return jnp.asarray(np.concatenate(parts, axis=0), BF16)


def _gla_level_map():
    t = np.arange(CHUNK)[:, None]
    s = np.arange(CHUNK)[None, :]
    out = np.full((CHUNK, CHUNK), -1, np.int32)
    out[t == s] = 0
    for level in range(GLA_LEVELS):
        n = (CHUNK // 2) >> level
        hit = (t // (2 * n) == s // (2 * n)) & (t % (2 * n) >= n) & (s % (2 * n) < n)
        out[hit] = 1 + level
    return jnp.asarray(out)


def _gla_kernel(q_ref, k_ref, v_ref, r_ref, la_ref, dm_ref, lvl_ref, og_ref, o_ref, state):
    @pl.when(pl.program_id(1) == 0)
    def _():
        state[...] = jnp.zeros_like(state)

    dm = dm_ref[...]
    lvl = lvl_ref[...]

    def chunk(g, carry):
        rows = pl.ds(pl.multiple_of(g * CHUNK, CHUNK), CHUNK)
        la = la_ref[rows, :]
        la_hi = la.astype(BF16)
        la_lo = (la - la_hi.astype(F32)).astype(BF16)
        decay = jnp.exp(_dot(dm, la_hi) + _dot(dm, la_lo))
        for h in range(GLA_HEADS):
            kc = slice(h * GLA_DK, (h + 1) * GLA_DK)
            vc = slice(h * GLA_DV, (h + 1) * GLA_DV)
            qh = q_ref[rows, kc].astype(F32)
            kh = k_ref[rows, kc].astype(F32)
            vh = v_ref[rows, vc]
            s_t = state[h]
            o = _dot_nt((qh * decay[0:CHUNK, kc]).astype(BF16), s_t.astype(BF16))
            att = jnp.where(lvl == 0, _dot_nt(qh.astype(BF16), kh.astype(BF16)), 0.0)
            for level in range(GLA_LEVELS):
                e = decay[(2 + level) * CHUNK:(3 + level) * CHUNK, kc]
                pair = _dot_nt((qh * e).astype(BF16), (kh * e).astype(BF16))
                att = jnp.where(lvl == 1 + level, pair, att)
            o = o + _dot(att.astype(BF16), vh)
            k_dec = (kh * decay[CHUNK:2 * CHUNK, kc]).astype(BF16)
            state[h] = (s_t * decay[CHUNK - 1:CHUNK, kc]
                        + lax.dot_general(vh, k_dec, _TN, preferred_element_type=F32))
            ms = jnp.mean(o * o, axis=-1, keepdims=True)
            on = o * lax.rsqrt(ms + RMS_EPS) * og_ref[:, vc]
            o_ref[rows, vc] = (jax.nn.silu(r_ref[rows, vc].astype(F32)) * on).astype(BF16)
        return carry

    lax.fori_loop(0, GLA_STEP_CHUNKS, chunk, 0)


def _gla(q, k, v, r, la, o_gain, batch, seq):
    tokens = q.shape[0]
    step = GLA_STEP_CHUNKS * CHUNK
    blocks = seq // step
    idx = lambda b, j: (b * blocks + j, 0)
    return pl.pallas_call(
        _gla_kernel,
        grid=(batch, blocks),
        in_specs=[pl.BlockSpec((step, GLA_KEY_WIDTH), idx), pl.BlockSpec((step, GLA_KEY_WIDTH), idx),
                  pl.BlockSpec((step, D_MODEL), idx), pl.BlockSpec((step, D_MODEL), idx),
                  pl.BlockSpec((step, GLA_KEY_WIDTH), idx),
                  _resident((8 * CHUNK, CHUNK)), _resident((CHUNK, CHUNK)), _resident((1, D_MODEL))],
        out_specs=pl.BlockSpec((step, D_MODEL), idx),
        out_shape=jax.ShapeDtypeStruct((tokens, D_MODEL), BF16),
        scratch_shapes=[pltpu.VMEM((GLA_HEADS, GLA_DV, GLA_DK), F32)],
        compiler_params=_params("parallel", "arbitrary"),
        name="gla_scan",
    )(q, k, v, r, la, _gla_decay_matrix(), _gla_level_map(),
      jnp.tile(o_gain, GLA_HEADS).reshape(1, D_MODEL))


def _sb_kernel(q_ref, k_ref, v_ref, tri_ref, o_ref, z_scr, w_scr, acc_scr, later_scr):
    qb = pl.program_id(2)
    visits = 2 * (qb + 1)
    last_block = k_ref.shape[0] // SB_KBLOCK - 1
    tri2 = tri_ref[...]
    lane = lax.broadcasted_iota(jnp.int32, (SB_QROWS, HEAD_PAIR), 1)
    low_half = lane < SB_HEAD_DIM
    q = q_ref[...]
    q_heads = (jnp.where(low_half, q, 0), jnp.where(low_half, 0, q))
    sign_bit = jnp.uint32(0x80000000)
    inv_ln2 = 1.0 / np.log(2.0)
    w_scr[...] = jnp.zeros_like(w_scr)
    acc_scr[...] = jnp.zeros_like(acc_scr)
    later_scr[...] = jnp.zeros_like(later_scr)

    def key_rows(kb):
        return pl.ds(pl.multiple_of(kb * SB_KBLOCK, SB_KBLOCK), SB_KBLOCK)

    def scores(kb, slot):
        kblk = k_ref[key_rows(kb), :]
        for h in range(2):
            z_scr[slot, h] = _dot_nt(q_heads[h], kblk)

    strips = [slice(r, r + SB_STRIP) for r in range(0, SB_QROWS, SB_STRIP)]

    def visit(v, slot, masked):
        kb = visits - 1 - v
        scores(jnp.maximum(kb - 1, 0), 1 - slot)
        vprev = v_ref[key_rows(jnp.minimum(kb + 1, last_block)), :]
        for h in range(2):
            acc_scr[h] += _dot(w_scr[h], vprev)
        if masked:
            t_idx = lax.broadcasted_iota(jnp.int32, (SB_STRIP, SB_KBLOCK), 0) + qb * SB_QROWS
            s_idx = lax.broadcasted_iota(jnp.int32, (SB_STRIP, SB_KBLOCK), 1) + kb * SB_KBLOCK
        incls = []
        for h in range(2):
            his, los = [], []
            for rows in strips:
                zr = z_scr[slot, h, rows, :]
                neg_abs = lax.bitcast_convert_type(lax.bitcast_convert_type(zr, jnp.uint32) | sign_bit, F32)
                sp = jnp.maximum(zr, 0.0) + jnp.log(1.0 + jnp.exp2(neg_abs)) * inv_ln2
                if masked:
                    sp = jnp.where(s_idx < t_idx + rows.start, sp, 0.0)
                hi = sp.astype(BF16)
                his.append(hi)
                los.append((sp - hi.astype(F32)).astype(BF16))
            split = jnp.concatenate([jnp.concatenate(his, axis=0), jnp.concatenate(los, axis=0)], axis=1)
            incls.append(_dot(split, tri2))
        for h in range(2):
            for rows in strips:
                incl = incls[h][rows]
                w = jnp.exp2(z_scr[slot, h, rows, :] - incl - later_scr[h, rows, :])
                if masked:
                    w = jnp.where(s_idx < t_idx + rows.start, w, 0.0)
                w_scr[h, rows, :] = w.astype(BF16)
                later_scr[h, rows, :] += incl[:, 0:1]

    scores(visits - 1, 0)
    visit(0, 0, True)
    visit(1, 1, True)

    def pair(i, carry):
        visit(2 + 2 * i, 0, False)
        visit(3 + 2 * i, 1, False)
        return carry

    lax.fori_loop(0, qb, pair, 0)
    first = v_ref[key_rows(0), :]
    out = [acc_scr[h] + _dot(w_scr[h], first) for h in range(2)]
    o_ref[...] = jnp.where(low_half, out[0], out[1]).astype(BF16)


def _stick_breaking(q, k, v, batch, seq):
    tokens = q.shape[0]
    blocks = seq // SB_QROWS
    pairs = D_MODEL // HEAD_PAIR
    j = np.arange(SB_KBLOCK)
    tri = j[:, None] >= j[None, :]
    tri2 = jnp.asarray(np.concatenate([tri, tri], axis=0), BF16)
    qo_spec = pl.BlockSpec((SB_QROWS, HEAD_PAIR), lambda b, h, i: (b * blocks + i, h))
    kv_spec = pl.BlockSpec((seq, HEAD_PAIR), lambda b, h, i: (b, h))
    return pl.pallas_call(
        _sb_kernel,
        grid=(batch, pairs, blocks),
        in_specs=[qo_spec, kv_spec, kv_spec, _resident((2 * SB_KBLOCK, SB_KBLOCK))],
        out_specs=qo_spec,
        out_shape=jax.ShapeDtypeStruct((tokens, D_MODEL), BF16),
        scratch_shapes=[pltpu.VMEM((2, 2, SB_QROWS, SB_KBLOCK), F32),
                        pltpu.VMEM((2, SB_QROWS, SB_KBLOCK), BF16),
                        pltpu.VMEM((2, SB_QROWS, HEAD_PAIR), F32),
                        pltpu.VMEM((2, SB_QROWS, 1), F32)],
        compiler_params=_params("parallel", "parallel", "arbitrary"),
        name="stick_breaking_attention",
    )(q, k, v, tri2)


def _post_kernel(o_ref, x_ref, mod_ref, gain_ref, wo_ref, w1_ref, w2_ref, out_ref):
    gate1 = mod_ref[0, 2:3, :]
    shift2, scale2, gate2 = mod_ref[0, 3:4, :], mod_ref[0, 4:5, :], mod_ref[0, 5:6, :]
    x1 = x_ref[...] + gate1 * _dot(o_ref[...], wo_ref[...])
    h = _modulated_norm(x1, gain_ref[...], scale2, shift2).astype(BF16)
    acc = jnp.zeros_like(x1)
    for c in range(FFN_HIDDEN // FFN_SLAB):
        t = _dot(h, w1_ref[:, c * FFN_SLAB:(c + 1) * FFN_SLAB])
        t = jnp.square(jnp.maximum(t, 0.0)).astype(BF16)
        acc = acc + _dot(t, w2_ref[c * FFN_SLAB:(c + 1) * FFN_SLAB, :])
    out_ref[...] = x1 + gate2 * acc


def _post(o, x2, mod, gain, w_out, w1, w2, seq):
    tokens = x2.shape[0]
    grid, x_spec, mod_spec = _row_specs(tokens, seq, TM_PROJ)
    width = o.shape[1]
    out_spec, out_shape = _row_out(tokens, D_MODEL, TM_PROJ, F32)
    return pl.pallas_call(
        _post_kernel,
        grid=grid,
        in_specs=[pl.BlockSpec((TM_PROJ, width), lambda i: (i, 0)), x_spec, mod_spec,
                  _resident((1, D_MODEL)), _resident((width, D_MODEL)),
                  _resident((D_MODEL, FFN_HIDDEN)), _resident((FFN_HIDDEN, D_MODEL))],
        out_specs=out_spec, out_shape=out_shape,
        compiler_params=_params("parallel"),
        name="outproj_mlp",
    )(o, x2, mod, gain.reshape(1, D_MODEL), w_out.astype(BF16), w1.astype(BF16), w2.astype(BF16))


def kernel(x, c, ada_w, ada_b, norm_mix, norm_ffn, ffn_w1, ffn_w2, a_w_in, a_q_gain, a_k_gain, a_rel_bias, a_w_out, b_w_in, b_v_gain, b_w_s, b_b_s, b_w_out, c_w_in, c_w_gate_up, c_b_gate, c_o_gain, c_w_out, d_w_in, d_w_out):
    batch, seq, _ = x.shape
    x2 = x.reshape(batch * seq, D_MODEL)
    mods = _modulation(c, ada_w, ada_b)
    for i in range(DEPTH):
        m, j = i % N_MIXERS, i // N_MIXERS
        mod = mods[i]
        if m == 0:
            q, k, v = _inproj_a(x2, mod, norm_mix[i], a_w_in[j], a_q_gain[j], a_k_gain[j], seq)
            o = _attn_a(q, k, v, a_rel_bias[j], batch, seq)
            w_out = a_w_out[j]
        elif m == 1:
            o = _mixb(x2, mod, norm_mix[i], b_w_in[j], b_v_gain[j], b_w_s[j], b_b_s[j], seq)
            w_out = b_w_out[j]
        elif m == 2:
            q, k, v, r, la = _inproj_c(x2, mod, norm_mix[i], c_w_in[j], c_w_gate_up[j], c_b_gate[j], seq)
            o = _gla(q, k, v, r, la, c_o_gain[j], batch, seq)
            w_out = c_w_out[j]
        else:
            q, k, v = _inproj_d(x2, mod, norm_mix[i], d_w_in[j], seq)
            o = _stick_breaking(q, k, v, batch, seq)
            w_out = d_w_out[j]
        x2 = _post(o, x2, mod, norm_ffn[i], w_out, ffn_w1[i], ffn_w2[i], seq)
    return x2.reshape(batch, seq, D_MODEL)
```

```python
import functools

import jax
import jax.numpy as jnp
import numpy as np
from jax import lax
from jax.experimental import pallas as pl
from jax.experimental.pallas import tpu as pltpu

F32 = jnp.float32
BF16 = jnp.bfloat16

D_MODEL = 1024
DEPTH = 4
N_MIXERS = 4
CHUNK = 64
RMS_EPS = 1e-6

A_HEADS = 16
A_HEAD_DIM = 64
A_LEFT_CHUNKS = 8
A_MAX_REL = 256
A_BAND = (A_LEFT_CHUNKS + 1) * CHUNK
A_BAND_PAD = 640
A_GROUP = 8 * CHUNK

SG_FF = 6 * D_MODEL
SG_HALF = SG_FF // 2
SG_GROUPS = 8
SG_WINDOW = 128
SG_GROUP_WIDTH = SG_HALF // SG_GROUPS

GLA_HEADS = 4
GLA_KEY_WIDTH = D_MODEL // 2
GLA_DK = 128
GLA_DV = 256
GLA_GATE_RANK = 16
GLA_GATE_PAD = 128
GLA_TAU = 16.0
GLA_LEVELS = 6
GLA_STEP_CHUNKS = 8
GLA_UNROLL = 2

SB_HEADS = 16
SB_HEAD_DIM = 64
SB_QROWS = 512
SB_KBLOCK = 256
SB_STRIP = 64
LOG2_E = float(np.log2(np.e))
SB_Q_SCALE = SB_HEAD_DIM ** -0.5 * LOG2_E
A_Q_SCALE = A_HEAD_DIM ** -0.5 * LOG2_E

FFN_HIDDEN = 4 * D_MODEL
FFN_SLAB = 1024

LANES = 128
HEAD_PAIR = LANES
VMEM_LIMIT_BYTES = 56 * 1024 * 1024

TM_PROJ = 512
TM_MIXB = 256

_NT = (((1,), (1,)), ((), ()))
_TN = (((0,), (0,)), ((), ()))


def _params(*semantics):
    return pltpu.CompilerParams(dimension_semantics=semantics,
                                vmem_limit_bytes=VMEM_LIMIT_BYTES)


def _dot(a, b):
    return jnp.dot(a, b, preferred_element_type=F32)


def _dot_nt(a, b):
    return lax.dot_general(a, b, _NT, preferred_element_type=F32)


def _resident(shape):
    zeros = (0,) * len(shape)
    return pl.BlockSpec(shape, lambda *_: zeros, pipeline_mode=pl.Buffered(1))


def _gelu(x):
    return 0.5 * x * (1.0 + lax.erf(x * (2.0 ** -0.5)))


def _modulated_norm(x, gain, scale, shift):
    ms = jnp.mean(x * x, axis=-1, keepdims=True)
    y = x * lax.rsqrt(ms + RMS_EPS) * gain
    return y * (1.0 + scale) + shift


def _mod_kernel(c_ref, w_ref, b_ref, o_ref):
    cond = jax.nn.silu(c_ref[...])
    o_ref[0] = jnp.dot(cond, w_ref[0], precision=lax.Precision.HIGHEST,
                       preferred_element_type=F32) + b_ref[0]


def _modulation(c, ada_w, ada_b):
    batch = c.shape[0]
    rows = 8
    c_pad = jnp.zeros((rows, D_MODEL), F32).at[:batch].set(c)
    tn = 1536
    out = pl.pallas_call(
        _mod_kernel,
        grid=(DEPTH, 6 * D_MODEL // tn),
        in_specs=[
            pl.BlockSpec((rows, D_MODEL), lambda i, j: (0, 0)),
            pl.BlockSpec((1, D_MODEL, tn), lambda i, j: (i, 0, j)),
            pl.BlockSpec((1, 1, tn), lambda i, j: (i, 0, j)),
        ],
        out_specs=pl.BlockSpec((1, rows, tn), lambda i, j: (i, 0, j)),
        out_shape=jax.ShapeDtypeStruct((DEPTH, rows, 6 * D_MODEL), F32),
        compiler_params=_params("parallel", "parallel"),
        name="adaln_modulation",
    )(c_pad, ada_w, ada_b.reshape(DEPTH, 1, 6 * D_MODEL))
    return out[:, :batch].reshape(DEPTH, batch, 6, D_MODEL)


def _normed_input(x_ref, mod_ref, gain_ref):
    shift, scale = mod_ref[0, 0:1, :], mod_ref[0, 1:2, :]
    return _modulated_norm(x_ref[...], gain_ref[...], scale, shift).astype(BF16)


def _inproj_a_kernel(x_ref, mod_ref, gain_ref, w_ref, qg_ref, kg_ref, seg_ref,
                     q_ref, k_ref, v_ref):
    h = _normed_input(x_ref, mod_ref, gain_ref)
    seg = seg_ref[...]

    def head_norm(y, gain):
        ssq = _dot((y * y).astype(BF16), seg)
        return y * lax.rsqrt(ssq * (1.0 / A_HEAD_DIM) + RMS_EPS) * gain

    width = seg.shape[0]
    for c in range(D_MODEL // width):
        cols = slice(c * width, (c + 1) * width)
        yq = _dot(h, w_ref[:, c * width:(c + 1) * width])
        q_ref[:, cols] = (head_norm(yq, qg_ref[:, cols]) * A_Q_SCALE).astype(BF16)
        yk = _dot(h, w_ref[:, D_MODEL + c * width:D_MODEL + (c + 1) * width])
        k_ref[:, cols] = head_norm(yk, kg_ref[:, cols]).astype(BF16)
        yv = _dot(h, w_ref[:, 2 * D_MODEL + c * width:2 * D_MODEL + (c + 1) * width])
        v_ref[:, cols] = yv.astype(BF16)


def _inproj_d_kernel(x_ref, mod_ref, gain_ref, w_ref, q_ref, k_ref, v_ref):
    h = _normed_input(x_ref, mod_ref, gain_ref)
    width = 512
    for c in range(D_MODEL // width):
        cols = slice(c * width, (c + 1) * width)
        yq = _dot(h, w_ref[:, c * width:(c + 1) * width])
        q_ref[:, cols] = (yq * SB_Q_SCALE).astype(BF16)
        yk = _dot(h, w_ref[:, D_MODEL + c * width:D_MODEL + (c + 1) * width])
        k_ref[:, cols] = yk.astype(BF16)
        yv = _dot(h, w_ref[:, 2 * D_MODEL + c * width:2 * D_MODEL + (c + 1) * width])
        v_ref[:, cols] = yv.astype(BF16)


def _inproj_c_kernel(x_ref, mod_ref, gain_ref, w_ref, wg_ref, bg_ref,
                     q_ref, k_ref, v_ref, r_ref, la_ref):
    h = _normed_input(x_ref, mod_ref, gain_ref)
    kw = GLA_KEY_WIDTH
    q_ref[...] = (_dot(h, w_ref[:, 0:kw]) * (GLA_DK ** -0.5)).astype(BF16)
    k_ref[...] = _dot(h, w_ref[:, kw:2 * kw]).astype(BF16)
    for c in range(2):
        cols = slice(c * kw, (c + 1) * kw)
        v_ref[:, cols] = _dot(h, w_ref[:, 2 * kw + c * kw:2 * kw + (c + 1) * kw]).astype(BF16)
        r_ref[:, cols] = _dot(h, w_ref[:, 4 * kw + c * kw:4 * kw + (c + 1) * kw]).astype(BF16)
    a = _dot(h, w_ref[:, 6 * kw:6 * kw + GLA_GATE_PAD]).astype(BF16)
    gate = _dot(a, wg_ref[...]) + bg_ref[...]
    la_ref[...] = jax.nn.log_sigmoid(gate) * (1.0 / GLA_TAU)


def _row_specs(tokens, seq, tm):
    tiles_per_batch = seq // tm
    x_spec = pl.BlockSpec((tm, D_MODEL), lambda i: (i, 0))
    mod_spec = pl.BlockSpec((1, 6, D_MODEL), lambda i: (i // tiles_per_batch, 0, 0))
    return (tokens // tm,), x_spec, mod_spec


def _row_out(tokens, width, tm, dtype=BF16):
    return (pl.BlockSpec((tm, width), lambda i: (i, 0)),
            jax.ShapeDtypeStruct((tokens, width), dtype))


def _inproj_a(x2, mod, gain, w_in, q_gain, k_gain, seq):
    tokens = x2.shape[0]
    grid, x_spec, mod_spec = _row_specs(tokens, seq, TM_PROJ)
    seg_width = 256
    seg_id = np.arange(seg_width) // A_HEAD_DIM
    seg = jnp.asarray(seg_id[:, None] == seg_id[None, :], BF16)
    specs, shapes = zip(*[_row_out(tokens, D_MODEL, TM_PROJ)] * 3)
    return pl.pallas_call(
        _inproj_a_kernel,
        grid=grid,
        in_specs=[x_spec, mod_spec, _resident((1, D_MODEL)), _resident((D_MODEL, 3 * D_MODEL)),
                  _resident((1, D_MODEL)), _resident((1, D_MODEL)),
                  _resident((seg_width, seg_width))],
        out_specs=list(specs), out_shape=list(shapes),
        compiler_params=_params("parallel"),
        name="inproj_relpos_attention",
    )(x2, mod, gain.reshape(1, D_MODEL), w_in.astype(BF16),
      jnp.tile(q_gain, A_HEADS).reshape(1, D_MODEL), jnp.tile(k_gain, A_HEADS).reshape(1, D_MODEL), seg)


def _inproj_d(x2, mod, gain, w_in, seq):
    tokens = x2.shape[0]
    grid, x_spec, mod_spec = _row_specs(tokens, seq, TM_PROJ)
    specs, shapes = zip(*[_row_out(tokens, D_MODEL, TM_PROJ)] * 3)
    return pl.pallas_call(
        _inproj_d_kernel,
        grid=grid,
        in_specs=[x_spec, mod_spec, _resident((1, D_MODEL)), _resident((D_MODEL, 3 * D_MODEL))],
        out_specs=list(specs), out_shape=list(shapes),
        compiler_params=_params("parallel"),
        name="inproj_stick_breaking",
    )(x2, mod, gain.reshape(1, D_MODEL), w_in.astype(BF16))


def _inproj_c(x2, mod, gain, w_in, w_gate_up, b_gate, seq):
    tokens = x2.shape[0]
    grid, x_spec, mod_spec = _row_specs(tokens, seq, TM_PROJ)
    main = 2 * GLA_KEY_WIDTH + 2 * D_MODEL
    w_pad = jnp.zeros((D_MODEL, main + GLA_GATE_PAD), BF16).at[:, :main + GLA_GATE_RANK].set(
        w_in.astype(BF16))
    wg_pad = jnp.zeros((GLA_GATE_PAD, GLA_KEY_WIDTH), BF16).at[:GLA_GATE_RANK].set(
        w_gate_up.astype(BF16))
    outs = [_row_out(tokens, GLA_KEY_WIDTH, TM_PROJ), _row_out(tokens, GLA_KEY_WIDTH, TM_PROJ),
            _row_out(tokens, D_MODEL, TM_PROJ), _row_out(tokens, D_MODEL, TM_PROJ),
            _row_out(tokens, GLA_KEY_WIDTH, TM_PROJ, F32)]
    specs, shapes = zip(*outs)
    return pl.pallas_call(
        _inproj_c_kernel,
        grid=grid,
        in_specs=[x_spec, mod_spec, _resident((1, D_MODEL)),
                  _resident((D_MODEL, main + GLA_GATE_PAD)),
                  _resident((GLA_GATE_PAD, GLA_KEY_WIDTH)), _resident((1, GLA_KEY_WIDTH))],
        out_specs=list(specs), out_shape=list(shapes),
        compiler_params=_params("parallel"),
        name="inproj_gla",
    )(x2, mod, gain.reshape(1, D_MODEL), w_pad, wg_pad, b_gate.reshape(1, GLA_KEY_WIDTH))


def _attn_a_kernel(q_ref, kp_ref, kc_ref, vp_ref, vc_ref, bias_ref, o_ref, kwin, vwin, bias_scr):
    block = pl.program_id(2)
    kwin[0:A_GROUP, :] = kp_ref[...]
    kwin[A_GROUP:2 * A_GROUP, :] = kc_ref[...]
    vwin[0:A_GROUP, :] = vp_ref[...]
    vwin[A_GROUP:2 * A_GROUP, :] = vc_ref[...]
    tail = jnp.zeros((A_BAND_PAD - A_BAND, HEAD_PAIR), BF16)
    kwin[2 * A_GROUP:, :] = tail
    vwin[2 * A_GROUP:, :] = tail

    lane = lax.broadcasted_iota(jnp.int32, (CHUNK, HEAD_PAIR), 1)
    low_half = lane < A_HEAD_DIM
    chunks = A_GROUP // CHUNK

    @pl.when(block == 0)
    def _():
        col = lax.broadcasted_iota(jnp.int32, (2 * CHUNK, A_BAND_PAD), 1)
        for c in range(chunks):
            bias_scr[c] = jnp.where(col >= A_GROUP - c * CHUNK, bias_ref[0], -jnp.inf)

    @pl.when(block == 1)
    def _():
        for c in range(chunks):
            bias_scr[c] = bias_ref[0]

    scores = []
    for c in range(chunks):
        qc = q_ref[c * CHUNK:(c + 1) * CHUNK, :]
        q2 = jnp.concatenate([jnp.where(low_half, qc, 0), jnp.where(low_half, 0, qc)], axis=0)
        scores.append(_dot_nt(q2, kwin[c * CHUNK:c * CHUNK + A_BAND_PAD, :]) + bias_scr[c])
    probs = [jnp.exp2(s - jnp.max(s, axis=-1, keepdims=True)) for s in scores]
    denoms = [jnp.sum(p, axis=-1, keepdims=True) for p in probs]
    outs = [_dot(p.astype(BF16), vwin[c * CHUNK:c * CHUNK + A_BAND_PAD, :]) / d
            for c, (p, d) in enumerate(zip(probs, denoms))]
    for c in range(chunks):
        o_ref[c * CHUNK:(c + 1) * CHUNK, :] = jnp.where(
            low_half, outs[c][0:CHUNK], outs[c][CHUNK:2 * CHUNK]).astype(BF16)


def _relpos_bias_table(rel_bias):
    rb = rel_bias.astype(F32) * LOG2_E
    far = A_MAX_REL + CHUNK - 1
    ext = jnp.concatenate([jnp.broadcast_to(rb[:, far:far + 1], (A_HEADS, far + 1)),
                           rb[:, far - 1::-1]], axis=1)
    table = jnp.stack([ext[:, CHUNK - 1 - i:CHUNK - 1 - i + A_BAND] for i in range(CHUNK)], axis=1)
    table = jnp.pad(table, ((0, 0), (0, 0), (0, A_BAND_PAD - A_BAND)), constant_values=-jnp.inf)
    return table.reshape(A_HEADS // 2, 2 * CHUNK, A_BAND_PAD)


def _attn_a(q, k, v, rel_bias, batch, seq):
    tokens = q.shape[0]
    blocks = seq // A_GROUP
    pairs = D_MODEL // HEAD_PAIR
    cur = lambda b, h, j: (b * blocks + j, h)
    prev = lambda b, h, j: (b * blocks + jnp.maximum(j - 1, 0), h)
    blk = (A_GROUP, HEAD_PAIR)
    win_rows = 2 * A_GROUP + A_BAND_PAD - A_BAND
    return pl.pallas_call(
        _attn_a_kernel,
        grid=(batch, pairs, blocks),
        in_specs=[pl.BlockSpec(blk, cur), pl.BlockSpec(blk, prev), pl.BlockSpec(blk, cur),
                  pl.BlockSpec(blk, prev), pl.BlockSpec(blk, cur),
                  pl.BlockSpec((1, 2 * CHUNK, A_BAND_PAD), lambda b, h, j: (h, 0, 0))],
        out_specs=pl.BlockSpec(blk, cur),
        out_shape=jax.ShapeDtypeStruct((tokens, D_MODEL), BF16),
        scratch_shapes=[pltpu.VMEM((win_rows, HEAD_PAIR), BF16),
                        pltpu.VMEM((win_rows, HEAD_PAIR), BF16),
                        pltpu.VMEM((A_GROUP // CHUNK, 2 * CHUNK, A_BAND_PAD), F32)],
        compiler_params=_params("parallel", "parallel", "arbitrary"),
        name="relpos_band_attention",
    )(q, k, k, v, v, _relpos_bias_table(rel_bias))


def _mixb_kernel(x_ref, mod_ref, gain_ref, w_ref, vg_ref, ws_ref, bs_ref, o_ref, v_scr):
    h = _normed_input(x_ref, mod_ref, gain_ref)
    tm = h.shape[0]
    slab = 512
    ssq = jnp.zeros((tm, 1), F32)
    for c in range(SG_HALF // slab):
        zc = _gelu(_dot(h, w_ref[:, SG_HALF + c * slab:SG_HALF + (c + 1) * slab]))
        ssq = ssq + jnp.sum(zc * zc, axis=-1, keepdims=True)
        v_scr[:, c * slab:(c + 1) * slab] = zc
    inv_rms = lax.rsqrt(ssq * (1.0 / SG_HALF) + RMS_EPS)

    row = lax.broadcasted_iota(jnp.int32, (SG_WINDOW, SG_WINDOW), 0) // CHUNK
    colc = lax.broadcasted_iota(jnp.int32, (SG_WINDOW, SG_WINDOW), 1) // CHUNK
    block_lower = row >= colc
    gw = SG_GROUP_WIDTH
    for g in range(SG_GROUPS):
        cols = slice(g * gw, (g + 1) * gw)
        w_s = jnp.where(block_lower, ws_ref[g], 0.0).astype(BF16)
        vn = (v_scr[:, cols] * inv_rms * vg_ref[:, cols]).astype(BF16)
        u = _gelu(_dot(h, w_ref[:, g * gw:(g + 1) * gw]))
        for w in range(tm // SG_WINDOW):
            rows = slice(w * SG_WINDOW, (w + 1) * SG_WINDOW)
            vm = _dot(w_s, vn[rows]) + bs_ref[g]
            o_ref[rows, cols] = (u[rows] * vm).astype(BF16)


def _mixb(x2, mod, gain, w_in, v_gain, w_s, b_s, seq):
    tokens = x2.shape[0]
    grid, x_spec, mod_spec = _row_specs(tokens, seq, TM_MIXB)
    bias = jnp.broadcast_to(b_s.astype(F32)[:, :, None], (SG_GROUPS, SG_WINDOW, SG_GROUP_WIDTH))
    out_spec, out_shape = _row_out(tokens, SG_HALF, TM_MIXB)
    return pl.pallas_call(
        _mixb_kernel,
        grid=grid,
        in_specs=[x_spec, mod_spec, _resident((1, D_MODEL)), _resident((D_MODEL, SG_FF)),
                  _resident((1, SG_HALF)), _resident((SG_GROUPS, SG_WINDOW, SG_WINDOW)),
                  _resident((SG_GROUPS, SG_WINDOW, SG_GROUP_WIDTH))],
        out_specs=out_spec, out_shape=out_shape,
        scratch_shapes=[pltpu.VMEM((TM_MIXB, SG_HALF), F32)],
        compiler_params=_params("parallel"),
        name="gmlp_spatial_gating",
    )(x2, mod, gain.reshape(1, D_MODEL), w_in.astype(BF16), v_gain.reshape(1, SG_HALF),
      w_s.astype(F32), bias)


def _gla_decay_matrix():
    r = np.arange(CHUNK)[:, None]
    i = np.arange(CHUNK)[None, :]
    parts = [i <= r, i > r]
    for level in range(GLA_LEVELS):
        n = (CHUNK // 2) >> level
        ref = (r // (2 * n)) * 2 * n + n
        upper = (r % (2 * n)) >= n
        parts.append(np.where(upper, (i > ref) & (i <= r), (i > r) & (i <= ref)))
    m = np.concatenate(parts, axis=0)
    return jnp.asarray(np.concatenate([m, m], axis=1), BF16)


def _gla_level_map():
    t = np.arange(CHUNK)[:, None]
    s = np.arange(CHUNK)[None, :]
    out = np.full((CHUNK, CHUNK), -1, np.int32)
    out[t == s] = 0
    for level in range(GLA_LEVELS):
        n = (CHUNK // 2) >> level
        hit = (t // (2 * n) == s // (2 * n)) & (t % (2 * n) >= n) & (s % (2 * n) < n)
        out[hit] = 1 + level
    return jnp.asarray(out)


def _gla_kernel(q_ref, k_ref, v_ref, r_ref, la_ref, dm_ref, lvl_ref, og_ref, o_ref, state):
    @pl.when(pl.program_id(1) == 0)
    def _():
        state[...] = jnp.zeros_like(state)

    dm = dm_ref[...]
    lvl = lvl_ref[...]

    heads = range(GLA_HEADS)
    kcols = [slice(h * GLA_DK, (h + 1) * GLA_DK) for h in heads]
    vcols = [slice(h * GLA_DV, (h + 1) * GLA_DV) for h in heads]

    def group(g, carry):
        base = g * (GLA_UNROLL * CHUNK)
        chunk_rows = [pl.ds(pl.multiple_of(base + u * CHUNK, CHUNK), CHUNK) for u in range(GLA_UNROLL)]
        decays = []
        for rows in chunk_rows:
            la = la_ref[rows, :]
            la_hi = la.astype(BF16)
            la_lo = (la - la_hi.astype(F32)).astype(BF16)
            decays.append(jnp.exp(_dot(dm, jnp.concatenate([la_hi, la_lo], axis=0))))
        units = [(u, h) for u in range(GLA_UNROLL) for h in heads]
        qs = [q_ref[chunk_rows[u], kcols[h]].astype(F32) for u, h in units]
        ks = [k_ref[chunk_rows[u], kcols[h]].astype(F32) for u, h in units]
        pairs = []
        for (u, h), qh, kh in zip(units, qs, ks):
            level_pairs = [_dot_nt(qh.astype(BF16), kh.astype(BF16))]
            for level in range(GLA_LEVELS):
                e = decays[u][(2 + level) * CHUNK:(3 + level) * CHUNK, kcols[h]]
                level_pairs.append(_dot_nt((qh * e).astype(BF16), (kh * e).astype(BF16)))
            pairs.append(level_pairs)
        intra = []
        for (u, h), level_pairs in zip(units, pairs):
            att = jnp.where(lvl == 0, level_pairs[0], 0.0)
            for level in range(GLA_LEVELS):
                att = jnp.where(lvl == 1 + level, level_pairs[1 + level], att)
            intra.append(_dot(att.astype(BF16), v_ref[chunk_rows[u], vcols[h]]))
        states = [state[h] for h in heads]
        for idx, (u, h) in enumerate(units):
            rows, kc, vc = chunk_rows[u], kcols[h], vcols[h]
            decay = decays[u]
            o = intra[idx] + _dot_nt((qs[idx] * decay[0:CHUNK, kc]).astype(BF16), states[h].astype(BF16))
            k_dec = (ks[idx] * decay[CHUNK:2 * CHUNK, kc]).astype(BF16)
            states[h] = (states[h] * decay[CHUNK - 1:CHUNK, kc]
                         + lax.dot_general(v_ref[rows, vc], k_dec, _TN, preferred_element_type=F32))
            ms = jnp.mean(o * o, axis=-1, keepdims=True)
            on = o * lax.rsqrt(ms + RMS_EPS) * og_ref[:, vc]
            o_ref[rows, vc] = (jax.nn.silu(r_ref[rows, vc].astype(F32)) * on).astype(BF16)
        for h in heads:
            state[h] = states[h]
        return carry

    lax.fori_loop(0, GLA_STEP_CHUNKS // GLA_UNROLL, group, 0)


def _gla(q, k, v, r, la, o_gain, batch, seq):
    tokens = q.shape[0]
    step = GLA_STEP_CHUNKS * CHUNK
    blocks = seq // step
    idx = lambda b, j: (b * blocks + j, 0)
    return pl.pallas_call(
        _gla_kernel,
        grid=(batch, blocks),
        in_specs=[pl.BlockSpec((step, GLA_KEY_WIDTH), idx), pl.BlockSpec((step, GLA_KEY_WIDTH), idx),
                  pl.BlockSpec((step, D_MODEL), idx), pl.BlockSpec((step, D_MODEL), idx),
                  pl.BlockSpec((step, GLA_KEY_WIDTH), idx),
                  _resident((8 * CHUNK, 2 * CHUNK)), _resident((CHUNK, CHUNK)), _resident((1, D_MODEL))],
        out_specs=pl.BlockSpec((step, D_MODEL), idx),
        out_shape=jax.ShapeDtypeStruct((tokens, D_MODEL), BF16),
        scratch_shapes=[pltpu.VMEM((GLA_HEADS, GLA_DV, GLA_DK), F32)],
        compiler_params=_params("parallel", "arbitrary"),
        name="gla_scan",
    )(q, k, v, r, la, _gla_decay_matrix(), _gla_level_map(),
      jnp.tile(o_gain, GLA_HEADS).reshape(1, D_MODEL))


def _sb_kernel(q_ref, k_ref, v_ref, tri_ref, o_ref, z_even, z_odd, w_scr, acc_scr, later_scr):
    z_scr = (z_even, z_odd)
    qb = pl.program_id(2)
    visits = 2 * (qb + 1)
    last_block = k_ref.shape[0] // SB_KBLOCK - 1
    tri = tri_ref[...]
    lane = lax.broadcasted_iota(jnp.int32, (SB_QROWS, HEAD_PAIR), 1)
    low_half = lane < SB_HEAD_DIM
    q = q_ref[...]
    q_heads = (jnp.where(low_half, q, 0), jnp.where(low_half, 0, q))
    sign_bit = jnp.uint32(0x80000000)
    inv_ln2 = 1.0 / np.log(2.0)
    w_scr[...] = jnp.zeros_like(w_scr)
    acc_scr[...] = jnp.zeros_like(acc_scr)
    later_scr[...] = jnp.zeros_like(later_scr)

    def key_rows(kb):
        return pl.ds(pl.multiple_of(kb * SB_KBLOCK, SB_KBLOCK), SB_KBLOCK)

    def scores(kb, slot):
        kblk = k_ref[key_rows(kb), :]
        for h in range(2):
            z_scr[slot][h] = _dot_nt(q_heads[h], kblk)

    strips = [slice(r, r + SB_STRIP) for r in range(0, SB_QROWS, SB_STRIP)]

    def visit(v, slot, masked):
        kb = visits - 1 - v
        scores(jnp.maximum(kb - 1, 0), 1 - slot)
        vprev = v_ref[key_rows(jnp.minimum(kb + 1, last_block)), :]
        for h in range(2):
            acc_scr[h] += _dot(w_scr[h], vprev)
        if masked:
            t_idx = lax.broadcasted_iota(jnp.int32, (SB_STRIP, SB_KBLOCK), 0) + qb * SB_QROWS
            s_idx = lax.broadcasted_iota(jnp.int32, (SB_STRIP, SB_KBLOCK), 1) + kb * SB_KBLOCK
        incls = []
        for h in range(2):
            sps = []
            for rows in strips:
                zr = z_scr[slot][h, rows, :]
                neg_abs = lax.bitcast_convert_type(lax.bitcast_convert_type(zr, jnp.uint32) | sign_bit, F32)
                sp = jnp.maximum(zr, 0.0) + jnp.log(1.0 + jnp.exp2(neg_abs)) * inv_ln2
                if masked:
                    sp = jnp.where(s_idx < t_idx + rows.start, sp, 0.0)
                sps.append(sp.astype(BF16))
            incls.append(_dot(jnp.concatenate(sps, axis=0), tri))
        for h in range(2):
            for rows in strips:
                incl = incls[h][rows]
                w = jnp.exp2(z_scr[slot][h, rows, :] - incl - later_scr[h, rows, :])
                if masked:
                    w = jnp.where(s_idx < t_idx + rows.start, w, 0.0)
                w_scr[h, rows, :] = w.astype(BF16)
                later_scr[h, rows, :] += incl[:, 0:1]

    scores(visits - 1, 0)
    visit(0, 0, True)
    visit(1, 1, True)

    def pair(v):
        visit(v, 0, False)
        visit(v + 1, 1, False)

    def quad(i, carry):
        pair(2 + 4 * i)
        pair(4 + 4 * i)
        return carry

    lax.fori_loop(0, qb // 2, quad, 0)

    @pl.when(qb % 2 == 1)
    def _():
        pair(2 * qb)
    first = v_ref[key_rows(0), :]
    out = [acc_scr[h] + _dot(w_scr[h], first) for h in range(2)]
    o_ref[...] = jnp.where(low_half, out[0], out[1]).astype(BF16)


def _stick_breaking(q, k, v, batch, seq):
    tokens = q.shape[0]
    blocks = seq // SB_QROWS
    pairs = D_MODEL // HEAD_PAIR
    j = np.arange(SB_KBLOCK)
    tri = jnp.asarray(j[:, None] >= j[None, :], BF16)
    qo_spec = pl.BlockSpec((SB_QROWS, HEAD_PAIR), lambda b, h, i: (b * blocks + i, h))
    kv_spec = pl.BlockSpec((seq, HEAD_PAIR), lambda b, h, i: (b, h))
    return pl.pallas_call(
        _sb_kernel,
        grid=(batch, pairs, blocks),
        in_specs=[qo_spec, kv_spec, kv_spec, _resident((SB_KBLOCK, SB_KBLOCK))],
        out_specs=qo_spec,
        out_shape=jax.ShapeDtypeStruct((tokens, D_MODEL), BF16),
        scratch_shapes=[pltpu.VMEM((2, SB_QROWS, SB_KBLOCK), F32),
                        pltpu.VMEM((2, SB_QROWS, SB_KBLOCK), F32),
                        pltpu.VMEM((2, SB_QROWS, SB_KBLOCK), BF16),
                        pltpu.VMEM((2, SB_QROWS, HEAD_PAIR), F32),
                        pltpu.VMEM((2, SB_QROWS, 1), F32)],
        compiler_params=_params("parallel", "parallel", "arbitrary"),
        name="stick_breaking_attention",
    )(q, k, v, tri)


def _post_kernel(o_ref, x_ref, mod_ref, gain_ref, wo_ref, w1_ref, w2_ref, out_ref):
    gate1 = mod_ref[0, 2:3, :]
    shift2, scale2, gate2 = mod_ref[0, 3:4, :], mod_ref[0, 4:5, :], mod_ref[0, 5:6, :]
    x1 = x_ref[...] + gate1 * _dot(o_ref[...], wo_ref[...])
    h = _modulated_norm(x1, gain_ref[...], scale2, shift2).astype(BF16)
    acc = jnp.zeros_like(x1)
    for c in range(FFN_HIDDEN // FFN_SLAB):
        t = _dot(h, w1_ref[:, c * FFN_SLAB:(c + 1) * FFN_SLAB])
        t = jnp.square(jnp.maximum(t, 0.0)).astype(BF16)
        acc = acc + _dot(t, w2_ref[c * FFN_SLAB:(c + 1) * FFN_SLAB, :])
    out_ref[...] = x1 + gate2 * acc


def _post(o, x2, mod, gain, w_out, w1, w2, seq):
    tokens = x2.shape[0]
    grid, x_spec, mod_spec = _row_specs(tokens, seq, TM_PROJ)
    width = o.shape[1]
    out_spec, out_shape = _row_out(tokens, D_MODEL, TM_PROJ, F32)
    return pl.pallas_call(
        _post_kernel,
        grid=grid,
        in_specs=[pl.BlockSpec((TM_PROJ, width), lambda i: (i, 0)), x_spec, mod_spec,
                  _resident((1, D_MODEL)), _resident((width, D_MODEL)),
                  _resident((D_MODEL, FFN_HIDDEN)), _resident((FFN_HIDDEN, D_MODEL))],
        out_specs=out_spec, out_shape=out_shape,
        compiler_params=_params("parallel"),
        name="outproj_mlp",
    )(o, x2, mod, gain.reshape(1, D_MODEL), w_out.astype(BF16), w1.astype(BF16), w2.astype(BF16))


def kernel(x, c, ada_w, ada_b, norm_mix, norm_ffn, ffn_w1, ffn_w2, a_w_in, a_q_gain, a_k_gain, a_rel_bias, a_w_out, b_w_in, b_v_gain, b_w_s, b_b_s, b_w_out, c_w_in, c_w_gate_up, c_b_gate, c_o_gain, c_w_out, d_w_in, d_w_out):
    batch, seq, _ = x.shape
    x2 = x.reshape(batch * seq, D_MODEL)
    mods = _modulation(c, ada_w, ada_b)
    for i in range(DEPTH):
        m, j = i % N_MIXERS, i // N_MIXERS
        mod = mods[i]
        if m == 0:
            q, k, v = _inproj_a(x2, mod, norm_mix[i], a_w_in[j], a_q_gain[j], a_k_gain[j], seq)
            o = _attn_a(q, k, v, a_rel_bias[j], batch, seq)
            w_out = a_w_out[j]
        elif m == 1:
            o = _mixb(x2, mod, norm_mix[i], b_w_in[j], b_v_gain[j], b_w_s[j], b_b_s[j], seq)
            w_out = b_w_out[j]
        elif m == 2:
            q, k, v, r, la = _inproj_c(x2, mod, norm_mix[i], c_w_in[j], c_w_gate_up[j], c_b_gate[j], seq)
            o = _gla(q, k, v, r, la, c_o_gain[j], batch, seq)
            w_out = c_w_out[j]
        else:
            q, k, v = _inproj_d(x2, mod, norm_mix[i], d_w_in[j], seq)
            o = _stick_breaking(q, k, v, batch, seq)
            w_out = d_w_out[j]
        x2 = _post(o, x2, mod, norm_ffn[i], w_out, ffn_w1[i], ffn_w2[i], seq)
    return x2.reshape(batch, seq, D_MODEL)
```

```python
import functools

import jax
import jax.numpy as jnp
import numpy as np
from jax import lax
from jax.experimental import pallas as pl
from jax.experimental.pallas import tpu as pltpu

F32 = jnp.float32
BF16 = jnp.bfloat16

D_MODEL = 1024
DEPTH = 4
N_MIXERS = 4
CHUNK = 64
RMS_EPS = 1e-6

A_HEADS = 16
A_HEAD_DIM = 64
A_LEFT_CHUNKS = 8
A_MAX_REL = 256
A_BAND = (A_LEFT_CHUNKS + 1) * CHUNK
A_BAND_PAD = 640
A_GROUP = 8 * CHUNK

SG_FF = 6 * D_MODEL
SG_HALF = SG_FF // 2
SG_GROUPS = 8
SG_WINDOW = 128
SG_GROUP_WIDTH = SG_HALF // SG_GROUPS

GLA_HEADS = 4
GLA_KEY_WIDTH = D_MODEL // 2
GLA_DK = 128
GLA_DV = 256
GLA_GATE_RANK = 16
GLA_GATE_PAD = 128
GLA_TAU = 16.0
GLA_LEVELS = 6
GLA_STEP_CHUNKS = 8
GLA_UNROLL = 2

SB_HEADS = 16
SB_HEAD_DIM = 64
SB_QROWS = 512
SB_KBLOCK = 256
SB_STRIP = 64
SB_DEAD_LOG2 = 160.0
LOG2_E = float(np.log2(np.e))
SB_Q_SCALE = SB_HEAD_DIM ** -0.5 * LOG2_E
A_Q_SCALE = A_HEAD_DIM ** -0.5 * LOG2_E

FFN_HIDDEN = 4 * D_MODEL
FFN_SLAB = 1024

LANES = 128
HEAD_PAIR = LANES
VMEM_LIMIT_BYTES = 56 * 1024 * 1024

TM_PROJ = 512
TM_MIXB = 256

_NT = (((1,), (1,)), ((), ()))
_TN = (((0,), (0,)), ((), ()))


def _params(*semantics):
    return pltpu.CompilerParams(dimension_semantics=semantics,
                                vmem_limit_bytes=VMEM_LIMIT_BYTES)


def _dot(a, b):
    return jnp.dot(a, b, preferred_element_type=F32)


def _dot_nt(a, b):
    return lax.dot_general(a, b, _NT, preferred_element_type=F32)


def _resident(shape):
    zeros = (0,) * len(shape)
    return pl.BlockSpec(shape, lambda *_: zeros, pipeline_mode=pl.Buffered(1))


def _gelu(x):
    return 0.5 * x * (1.0 + lax.erf(x * (2.0 ** -0.5)))


def _modulated_norm(x, gain, scale, shift):
    ms = jnp.mean(x * x, axis=-1, keepdims=True)
    y = x * lax.rsqrt(ms + RMS_EPS) * gain
    return y * (1.0 + scale) + shift


def _mod_kernel(c_ref, w_ref, b_ref, o_ref):
    cond = jax.nn.silu(c_ref[...])
    o_ref[0] = jnp.dot(cond, w_ref[0], precision=lax.Precision.HIGHEST,
                       preferred_element_type=F32) + b_ref[0]


def _modulation(c, ada_w, ada_b):
    batch = c.shape[0]
    rows = 8
    c_pad = jnp.zeros((rows, D_MODEL), F32).at[:batch].set(c)
    tn = 1536
    out = pl.pallas_call(
        _mod_kernel,
        grid=(DEPTH, 6 * D_MODEL // tn),
        in_specs=[
            pl.BlockSpec((rows, D_MODEL), lambda i, j: (0, 0)),
            pl.BlockSpec((1, D_MODEL, tn), lambda i, j: (i, 0, j)),
            pl.BlockSpec((1, 1, tn), lambda i, j: (i, 0, j)),
        ],
        out_specs=pl.BlockSpec((1, rows, tn), lambda i, j: (i, 0, j)),
        out_shape=jax.ShapeDtypeStruct((DEPTH, rows, 6 * D_MODEL), F32),
        compiler_params=_params("parallel", "parallel"),
        name="adaln_modulation",
    )(c_pad, ada_w, ada_b.reshape(DEPTH, 1, 6 * D_MODEL))
    return out[:, :batch].reshape(DEPTH, batch, 6, D_MODEL)


def _normed_input(x_ref, mod_ref, gain_ref):
    shift, scale = mod_ref[0, 0:1, :], mod_ref[0, 1:2, :]
    return _modulated_norm(x_ref[...], gain_ref[...], scale, shift).astype(BF16)


def _inproj_a_kernel(x_ref, mod_ref, gain_ref, w_ref, qg_ref, kg_ref, seg_ref,
                     q_ref, k_ref, v_ref):
    h = _normed_input(x_ref, mod_ref, gain_ref)
    seg = seg_ref[...]

    def head_norm(y, gain):
        ssq = _dot((y * y).astype(BF16), seg)
        return y * lax.rsqrt(ssq * (1.0 / A_HEAD_DIM) + RMS_EPS) * gain

    width = seg.shape[0]
    for c in range(D_MODEL // width):
        cols = slice(c * width, (c + 1) * width)
        yq = _dot(h, w_ref[:, c * width:(c + 1) * width])
        q_ref[:, cols] = (head_norm(yq, qg_ref[:, cols]) * A_Q_SCALE).astype(BF16)
        yk = _dot(h, w_ref[:, D_MODEL + c * width:D_MODEL + (c + 1) * width])
        k_ref[:, cols] = head_norm(yk, kg_ref[:, cols]).astype(BF16)
        yv = _dot(h, w_ref[:, 2 * D_MODEL + c * width:2 * D_MODEL + (c + 1) * width])
        v_ref[:, cols] = yv.astype(BF16)


def _inproj_d_kernel(x_ref, mod_ref, gain_ref, w_ref, q_ref, k_ref, v_ref):
    h = _normed_input(x_ref, mod_ref, gain_ref)
    width = 512
    for c in range(D_MODEL // width):
        cols = slice(c * width, (c + 1) * width)
        yq = _dot(h, w_ref[:, c * width:(c + 1) * width])
        q_ref[:, cols] = (yq * SB_Q_SCALE).astype(BF16)
        yk = _dot(h, w_ref[:, D_MODEL + c * width:D_MODEL + (c + 1) * width])
        k_ref[:, cols] = yk.astype(BF16)
        yv = _dot(h, w_ref[:, 2 * D_MODEL + c * width:2 * D_MODEL + (c + 1) * width])
        v_ref[:, cols] = yv.astype(BF16)


def _inproj_c_kernel(x_ref, mod_ref, gain_ref, w_ref, wg_ref, bg_ref,
                     q_ref, k_ref, v_ref, r_ref, la_ref):
    h = _normed_input(x_ref, mod_ref, gain_ref)
    kw = GLA_KEY_WIDTH
    q_ref[...] = (_dot(h, w_ref[:, 0:kw]) * (GLA_DK ** -0.5)).astype(BF16)
    k_ref[...] = _dot(h, w_ref[:, kw:2 * kw]).astype(BF16)
    for c in range(2):
        cols = slice(c * kw, (c + 1) * kw)
        v_ref[:, cols] = _dot(h, w_ref[:, 2 * kw + c * kw:2 * kw + (c + 1) * kw]).astype(BF16)
        r_ref[:, cols] = _dot(h, w_ref[:, 4 * kw + c * kw:4 * kw + (c + 1) * kw]).astype(BF16)
    a = _dot(h, w_ref[:, 6 * kw:6 * kw + GLA_GATE_PAD]).astype(BF16)
    gate = _dot(a, wg_ref[...]) + bg_ref[...]
    la_ref[...] = jax.nn.log_sigmoid(gate) * (1.0 / GLA_TAU)


def _row_specs(tokens, seq, tm):
    tiles_per_batch = seq // tm
    x_spec = pl.BlockSpec((tm, D_MODEL), lambda i: (i, 0))
    mod_spec = pl.BlockSpec((1, 6, D_MODEL), lambda i: (i // tiles_per_batch, 0, 0))
    return (tokens // tm,), x_spec, mod_spec


def _row_out(tokens, width, tm, dtype=BF16):
    return (pl.BlockSpec((tm, width), lambda i: (i, 0)),
            jax.ShapeDtypeStruct((tokens, width), dtype))


def _inproj_a(x2, mod, gain, w_in, q_gain, k_gain, seq):
    tokens = x2.shape[0]
    grid, x_spec, mod_spec = _row_specs(tokens, seq, TM_PROJ)
    seg_width = 256
    seg_id = np.arange(seg_width) // A_HEAD_DIM
    seg = jnp.asarray(seg_id[:, None] == seg_id[None, :], BF16)
    specs, shapes = zip(*[_row_out(tokens, D_MODEL, TM_PROJ)] * 3)
    return pl.pallas_call(
        _inproj_a_kernel,
        grid=grid,
        in_specs=[x_spec, mod_spec, _resident((1, D_MODEL)), _resident((D_MODEL, 3 * D_MODEL)),
                  _resident((1, D_MODEL)), _resident((1, D_MODEL)),
                  _resident((seg_width, seg_width))],
        out_specs=list(specs), out_shape=list(shapes),
        compiler_params=_params("parallel"),
        name="inproj_relpos_attention",
    )(x2, mod, gain.reshape(1, D_MODEL), w_in.astype(BF16),
      jnp.tile(q_gain, A_HEADS).reshape(1, D_MODEL), jnp.tile(k_gain, A_HEADS).reshape(1, D_MODEL), seg)


def _inproj_d(x2, mod, gain, w_in, seq):
    tokens = x2.shape[0]
    grid, x_spec, mod_spec = _row_specs(tokens, seq, TM_PROJ)
    specs, shapes = zip(*[_row_out(tokens, D_MODEL, TM_PROJ)] * 3)
    return pl.pallas_call(
        _inproj_d_kernel,
        grid=grid,
        in_specs=[x_spec, mod_spec, _resident((1, D_MODEL)), _resident((D_MODEL, 3 * D_MODEL))],
        out_specs=list(specs), out_shape=list(shapes),
        compiler_params=_params("parallel"),
        name="inproj_stick_breaking",
    )(x2, mod, gain.reshape(1, D_MODEL), w_in.astype(BF16))


def _inproj_c(x2, mod, gain, w_in, w_gate_up, b_gate, seq):
    tokens = x2.shape[0]
    grid, x_spec, mod_spec = _row_specs(tokens, seq, TM_PROJ)
    main = 2 * GLA_KEY_WIDTH + 2 * D_MODEL
    w_pad = jnp.zeros((D_MODEL, main + GLA_GATE_PAD), BF16).at[:, :main + GLA_GATE_RANK].set(
        w_in.astype(BF16))
    wg_pad = jnp.zeros((GLA_GATE_PAD, GLA_KEY_WIDTH), BF16).at[:GLA_GATE_RANK].set(
        w_gate_up.astype(BF16))
    outs = [_row_out(tokens, GLA_KEY_WIDTH, TM_PROJ), _row_out(tokens, GLA_KEY_WIDTH, TM_PROJ),
            _row_out(tokens, D_MODEL, TM_PROJ), _row_out(tokens, D_MODEL, TM_PROJ),
            _row_out(tokens, GLA_KEY_WIDTH, TM_PROJ, F32)]
    specs, shapes = zip(*outs)
    return pl.pallas_call(
        _inproj_c_kernel,
        grid=grid,
        in_specs=[x_spec, mod_spec, _resident((1, D_MODEL)),
                  _resident((D_MODEL, main + GLA_GATE_PAD)),
                  _resident((GLA_GATE_PAD, GLA_KEY_WIDTH)), _resident((1, GLA_KEY_WIDTH))],
        out_specs=list(specs), out_shape=list(shapes),
        compiler_params=_params("parallel"),
        name="inproj_gla",
    )(x2, mod, gain.reshape(1, D_MODEL), w_pad, wg_pad, b_gate.reshape(1, GLA_KEY_WIDTH))


def _attn_a_kernel(q_ref, kp_ref, kc_ref, vp_ref, vc_ref, bias_ref, o_ref, kwin, vwin, bias_scr):
    block = pl.program_id(2)
    kwin[0:A_GROUP, :] = kp_ref[...]
    kwin[A_GROUP:2 * A_GROUP, :] = kc_ref[...]
    vwin[0:A_GROUP, :] = vp_ref[...]
    vwin[A_GROUP:2 * A_GROUP, :] = vc_ref[...]
    tail = jnp.zeros((A_BAND_PAD - A_BAND, HEAD_PAIR), BF16)
    kwin[2 * A_GROUP:, :] = tail
    vwin[2 * A_GROUP:, :] = tail

    lane = lax.broadcasted_iota(jnp.int32, (CHUNK, HEAD_PAIR), 1)
    low_half = lane < A_HEAD_DIM
    chunks = A_GROUP // CHUNK

    @pl.when(block == 0)
    def _():
        col = lax.broadcasted_iota(jnp.int32, (2 * CHUNK, A_BAND_PAD), 1)
        for c in range(chunks):
            bias_scr[c] = jnp.where(col >= A_GROUP - c * CHUNK, bias_ref[0], -jnp.inf)

    @pl.when(block == 1)
    def _():
        for c in range(chunks):
            bias_scr[c] = bias_ref[0]

    scores = []
    for c in range(chunks):
        qc = q_ref[c * CHUNK:(c + 1) * CHUNK, :]
        q2 = jnp.concatenate([jnp.where(low_half, qc, 0), jnp.where(low_half, 0, qc)], axis=0)
        scores.append(_dot_nt(q2, kwin[c * CHUNK:c * CHUNK + A_BAND_PAD, :]) + bias_scr[c])
    maxes = [jnp.max(s, axis=-1, keepdims=True) for s in scores]
    probs, denoms = [], []
    for s, m in zip(scores, maxes):
        p = jnp.exp2(s - m)
        denoms.append(jnp.sum(p, axis=-1, keepdims=True))
        probs.append(p.astype(BF16))
    outs = [_dot(p, vwin[c * CHUNK:c * CHUNK + A_BAND_PAD, :]) / d
            for c, (p, d) in enumerate(zip(probs, denoms))]
    for c in range(chunks):
        o_ref[c * CHUNK:(c + 1) * CHUNK, :] = jnp.where(
            low_half, outs[c][0:CHUNK], outs[c][CHUNK:2 * CHUNK]).astype(BF16)


def _relpos_bias_table(rel_bias):
    rb = rel_bias.astype(F32) * LOG2_E
    far = A_MAX_REL + CHUNK - 1
    ext = jnp.concatenate([jnp.broadcast_to(rb[:, far:far + 1], (A_HEADS, far + 1)),
                           rb[:, far - 1::-1]], axis=1)
    table = jnp.stack([ext[:, CHUNK - 1 - i:CHUNK - 1 - i + A_BAND] for i in range(CHUNK)], axis=1)
    table = jnp.pad(table, ((0, 0), (0, 0), (0, A_BAND_PAD - A_BAND)), constant_values=-jnp.inf)
    return table.reshape(A_HEADS // 2, 2 * CHUNK, A_BAND_PAD)


def _attn_a(q, k, v, rel_bias, batch, seq):
    tokens = q.shape[0]
    blocks = seq // A_GROUP
    pairs = D_MODEL // HEAD_PAIR
    cur = lambda b, h, j: (b * blocks + j, h)
    prev = lambda b, h, j: (b * blocks + jnp.maximum(j - 1, 0), h)
    blk = (A_GROUP, HEAD_PAIR)
    win_rows = 2 * A_GROUP + A_BAND_PAD - A_BAND
    return pl.pallas_call(
        _attn_a_kernel,
        grid=(batch, pairs, blocks),
        in_specs=[pl.BlockSpec(blk, cur), pl.BlockSpec(blk, prev), pl.BlockSpec(blk, cur),
                  pl.BlockSpec(blk, prev), pl.BlockSpec(blk, cur),
                  pl.BlockSpec((1, 2 * CHUNK, A_BAND_PAD), lambda b, h, j: (h, 0, 0))],
        out_specs=pl.BlockSpec(blk, cur),
        out_shape=jax.ShapeDtypeStruct((tokens, D_MODEL), BF16),
        scratch_shapes=[pltpu.VMEM((win_rows, HEAD_PAIR), BF16),
                        pltpu.VMEM((win_rows, HEAD_PAIR), BF16),
                        pltpu.VMEM((A_GROUP // CHUNK, 2 * CHUNK, A_BAND_PAD), F32)],
        compiler_params=_params("parallel", "parallel", "arbitrary"),
        name="relpos_band_attention",
    )(q, k, k, v, v, _relpos_bias_table(rel_bias))


def _mixb_kernel(x_ref, mod_ref, gain_ref, w_ref, vg_ref, ws_ref, bs_ref, o_ref, v_scr):
    h = _normed_input(x_ref, mod_ref, gain_ref)
    tm = h.shape[0]
    slab = 512
    ssq = jnp.zeros((tm, 1), F32)
    for c in range(SG_HALF // slab):
        zc = _gelu(_dot(h, w_ref[:, SG_HALF + c * slab:SG_HALF + (c + 1) * slab]))
        ssq = ssq + jnp.sum(zc * zc, axis=-1, keepdims=True)
        v_scr[:, c * slab:(c + 1) * slab] = zc
    inv_rms = lax.rsqrt(ssq * (1.0 / SG_HALF) + RMS_EPS)

    row = lax.broadcasted_iota(jnp.int32, (SG_WINDOW, SG_WINDOW), 0) // CHUNK
    colc = lax.broadcasted_iota(jnp.int32, (SG_WINDOW, SG_WINDOW), 1) // CHUNK
    block_lower = row >= colc
    gw = SG_GROUP_WIDTH
    for g in range(SG_GROUPS):
        cols = slice(g * gw, (g + 1) * gw)
        w_s = jnp.where(block_lower, ws_ref[g], 0.0).astype(BF16)
        vn = (v_scr[:, cols] * inv_rms * vg_ref[:, cols]).astype(BF16)
        u = _gelu(_dot(h, w_ref[:, g * gw:(g + 1) * gw]))
        for w in range(tm // SG_WINDOW):
            rows = slice(w * SG_WINDOW, (w + 1) * SG_WINDOW)
            vm = _dot(w_s, vn[rows]) + bs_ref[g]
            o_ref[rows, cols] = (u[rows] * vm).astype(BF16)


def _mixb(x2, mod, gain, w_in, v_gain, w_s, b_s, seq):
    tokens = x2.shape[0]
    grid, x_spec, mod_spec = _row_specs(tokens, seq, TM_MIXB)
    bias = jnp.broadcast_to(b_s.astype(F32)[:, :, None], (SG_GROUPS, SG_WINDOW, SG_GROUP_WIDTH))
    out_spec, out_shape = _row_out(tokens, SG_HALF, TM_MIXB)
    return pl.pallas_call(
        _mixb_kernel,
        grid=grid,
        in_specs=[x_spec, mod_spec, _resident((1, D_MODEL)), _resident((D_MODEL, SG_FF)),
                  _resident((1, SG_HALF)), _resident((SG_GROUPS, SG_WINDOW, SG_WINDOW)),
                  _resident((SG_GROUPS, SG_WINDOW, SG_GROUP_WIDTH))],
        out_specs=out_spec, out_shape=out_shape,
        scratch_shapes=[pltpu.VMEM((TM_MIXB, SG_HALF), F32)],
        compiler_params=_params("parallel"),
        name="gmlp_spatial_gating",
    )(x2, mod, gain.reshape(1, D_MODEL), w_in.astype(BF16), v_gain.reshape(1, SG_HALF),
      w_s.astype(F32), bias)


def _gla_decay_matrix():
    r = np.arange(CHUNK)[:, None]
    i = np.arange(CHUNK)[None, :]
    parts = [i <= r, i > r]
    for level in range(GLA_LEVELS):
        n = (CHUNK // 2) >> level
        ref = (r // (2 * n)) * 2 * n + n
        upper = (r % (2 * n)) >= n
        parts.append(np.where(upper, (i > ref) & (i <= r), (i > r) & (i <= ref)))
    m = np.concatenate(parts, axis=0)
    return jnp.asarray(np.concatenate([m, m], axis=1), BF16)


def _gla_level_map():
    t = np.arange(CHUNK)[:, None]
    s = np.arange(CHUNK)[None, :]
    out = np.full((CHUNK, CHUNK), -1, np.int32)
    out[t == s] = 0
    for level in range(GLA_LEVELS):
        n = (CHUNK // 2) >> level
        hit = (t // (2 * n) == s // (2 * n)) & (t % (2 * n) >= n) & (s % (2 * n) < n)
        out[hit] = 1 + level
    return jnp.asarray(out)


def _gla_kernel(q_ref, k_ref, v_ref, r_ref, la_ref, dm_ref, lvl_ref, og_ref, o_ref, state):
    @pl.when(pl.program_id(1) == 0)
    def _():
        state[...] = jnp.zeros_like(state)

    dm = dm_ref[...]
    lvl = lvl_ref[...]

    heads = range(GLA_HEADS)
    kcols = [slice(h * GLA_DK, (h + 1) * GLA_DK) for h in heads]
    vcols = [slice(h * GLA_DV, (h + 1) * GLA_DV) for h in heads]

    def group(g, carry):
        base = g * (GLA_UNROLL * CHUNK)
        chunk_rows = [pl.ds(pl.multiple_of(base + u * CHUNK, CHUNK), CHUNK) for u in range(GLA_UNROLL)]
        decays = []
        for rows in chunk_rows:
            la = la_ref[rows, :]
            la_hi = la.astype(BF16)
            la_lo = (la - la_hi.astype(F32)).astype(BF16)
            decays.append(jnp.exp(_dot(dm, jnp.concatenate([la_hi, la_lo], axis=0))))
        units = [(u, h) for u in range(GLA_UNROLL) for h in heads]
        qs = [q_ref[chunk_rows[u], kcols[h]].astype(F32) for u, h in units]
        ks = [k_ref[chunk_rows[u], kcols[h]].astype(F32) for u, h in units]
        pairs = []
        for (u, h), qh, kh in zip(units, qs, ks):
            level_pairs = [_dot_nt(qh.astype(BF16), kh.astype(BF16))]
            for level in range(GLA_LEVELS):
                e = decays[u][(2 + level) * CHUNK:(3 + level) * CHUNK, kcols[h]]
                level_pairs.append(_dot_nt((qh * e).astype(BF16), (kh * e).astype(BF16)))
            pairs.append(level_pairs)
        intra = []
        for (u, h), level_pairs in zip(units, pairs):
            att = jnp.where(lvl == 0, level_pairs[0], 0.0)
            for level in range(GLA_LEVELS):
                att = jnp.where(lvl == 1 + level, level_pairs[1 + level], att)
            intra.append(_dot(att.astype(BF16), v_ref[chunk_rows[u], vcols[h]]))
        states = [state[h] for h in heads]
        for idx, (u, h) in enumerate(units):
            rows, kc, vc = chunk_rows[u], kcols[h], vcols[h]
            decay = decays[u]
            o = intra[idx] + _dot_nt((qs[idx] * decay[0:CHUNK, kc]).astype(BF16), states[h].astype(BF16))
            k_dec = (ks[idx] * decay[CHUNK:2 * CHUNK, kc]).astype(BF16)
            states[h] = (states[h] * decay[CHUNK - 1:CHUNK, kc]
                         + lax.dot_general(v_ref[rows, vc], k_dec, _TN, preferred_element_type=F32))
            ms = jnp.mean(o * o, axis=-1, keepdims=True)
            on = o * lax.rsqrt(ms + RMS_EPS) * og_ref[:, vc]
            o_ref[rows, vc] = (jax.nn.silu(r_ref[rows, vc].astype(F32)) * on).astype(BF16)
        for h in heads:
            state[h] = states[h]
        return carry

    lax.fori_loop(0, GLA_STEP_CHUNKS // GLA_UNROLL, group, 0)


def _gla(q, k, v, r, la, o_gain, batch, seq):
    tokens = q.shape[0]
    step = GLA_STEP_CHUNKS * CHUNK
    blocks = seq // step
    idx = lambda b, j: (b * blocks + j, 0)
    return pl.pallas_call(
        _gla_kernel,
        grid=(batch, blocks),
        in_specs=[pl.BlockSpec((step, GLA_KEY_WIDTH), idx), pl.BlockSpec((step, GLA_KEY_WIDTH), idx),
                  pl.BlockSpec((step, D_MODEL), idx), pl.BlockSpec((step, D_MODEL), idx),
                  pl.BlockSpec((step, GLA_KEY_WIDTH), idx),
                  _resident((8 * CHUNK, 2 * CHUNK)), _resident((CHUNK, CHUNK)), _resident((1, D_MODEL))],
        out_specs=pl.BlockSpec((step, D_MODEL), idx),
        out_shape=jax.ShapeDtypeStruct((tokens, D_MODEL), BF16),
        scratch_shapes=[pltpu.VMEM((GLA_HEADS, GLA_DV, GLA_DK), F32)],
        compiler_params=_params("parallel", "arbitrary"),
        name="gla_scan",
    )(q, k, v, r, la, _gla_decay_matrix(), _gla_level_map(),
      jnp.tile(o_gain, GLA_HEADS).reshape(1, D_MODEL))


def _sb_kernel(q_ref, k_ref, v_ref, tri_ref, o_ref, z_scr, w_scr, acc_scr, later_scr):
    qb = pl.program_id(2)
    visits = 2 * (qb + 1)
    last_block = k_ref.shape[0] // SB_KBLOCK - 1
    tri = tri_ref[...]
    lane = lax.broadcasted_iota(jnp.int32, (SB_QROWS, HEAD_PAIR), 1)
    low_half = lane < SB_HEAD_DIM
    q = q_ref[...]
    q_heads = (jnp.where(low_half, q, 0), jnp.where(low_half, 0, q))
    sign_bit = jnp.uint32(0x80000000)
    inv_ln2 = 1.0 / np.log(2.0)
    w_scr[...] = jnp.zeros_like(w_scr)
    acc_scr[...] = jnp.zeros_like(acc_scr)
    later_scr[...] = jnp.zeros_like(later_scr)

    def key_rows(kb):
        return pl.ds(pl.multiple_of(kb * SB_KBLOCK, SB_KBLOCK), SB_KBLOCK)

    def scores(kb, slot):
        kblk = k_ref[key_rows(kb), :]
        for h in range(2):
            z_scr[slot, h] = _dot_nt(q_heads[h], kblk)

    strips = [slice(r, r + SB_STRIP) for r in range(0, SB_QROWS, SB_STRIP)]

    def visit(v, slot, masked):
        kb = visits - 1 - v
        scores(jnp.maximum(kb - 1, 0), 1 - slot)
        vprev = v_ref[key_rows(jnp.minimum(kb + 1, last_block)), :]
        for h in range(2):
            acc_scr[h] += _dot(w_scr[h], vprev)
        if masked:
            t_idx = lax.broadcasted_iota(jnp.int32, (SB_STRIP, SB_KBLOCK), 0) + qb * SB_QROWS
            s_idx = lax.broadcasted_iota(jnp.int32, (SB_STRIP, SB_KBLOCK), 1) + kb * SB_KBLOCK
        incls = []
        for h in range(2):
            sps = []
            for rows in strips:
                zr = z_scr[slot, h, rows, :]
                neg_abs = lax.bitcast_convert_type(lax.bitcast_convert_type(zr, jnp.uint32) | sign_bit, F32)
                sp = jnp.maximum(zr, 0.0) + jnp.log(1.0 + jnp.exp2(neg_abs)) * inv_ln2
                if masked:
                    sp = jnp.where(s_idx < t_idx + rows.start, sp, 0.0)
                sps.append(sp.astype(BF16))
            incls.append(_dot(jnp.concatenate(sps, axis=0), tri))
        for h in range(2):
            for rows in strips:
                incl = incls[h][rows]
                w = jnp.exp2(z_scr[slot, h, rows, :] - incl - later_scr[h, rows, :])
                if masked:
                    w = jnp.where(s_idx < t_idx + rows.start, w, 0.0)
                w_scr[h, rows, :] = w.astype(BF16)
                later_scr[h, rows, :] += incl[:, 0:1]

    def finished(v):
        return jnp.logical_or(v >= visits, jnp.min(later_scr[...]) >= SB_DEAD_LOG2).astype(jnp.int32)

    scores(visits - 1, 0)
    visit(0, 0, True)
    visit(1, 1, True)

    def step(carry):
        v, _ = carry
        visit(v, v % 2, False)
        return v + 1, finished(v + 1)

    v_end, _ = lax.while_loop(lambda carry: carry[1] == 0, step, (jnp.int32(2), finished(2)))
    pending = v_ref[key_rows(visits - v_end), :]
    out = [acc_scr[h] + _dot(w_scr[h], pending) for h in range(2)]
    o_ref[...] = jnp.where(low_half, out[0], out[1]).astype(BF16)


def _stick_breaking(q, k, v, batch, seq):
    tokens = q.shape[0]
    blocks = seq // SB_QROWS
    pairs = D_MODEL // HEAD_PAIR
    j = np.arange(SB_KBLOCK)
    tri = jnp.asarray(j[:, None] >= j[None, :], BF16)
    qo_spec = pl.BlockSpec((SB_QROWS, HEAD_PAIR), lambda b, h, i: (b * blocks + i, h))
    kv_spec = pl.BlockSpec((seq, HEAD_PAIR), lambda b, h, i: (b, h))
    return pl.pallas_call(
        _sb_kernel,
        grid=(batch, pairs, blocks),
        in_specs=[qo_spec, kv_spec, kv_spec, _resident((SB_KBLOCK, SB_KBLOCK))],
        out_specs=qo_spec,
        out_shape=jax.ShapeDtypeStruct((tokens, D_MODEL), BF16),
        scratch_shapes=[pltpu.VMEM((2, 2, SB_QROWS, SB_KBLOCK), F32),
                        pltpu.VMEM((2, SB_QROWS, SB_KBLOCK), BF16),
                        pltpu.VMEM((2, SB_QROWS, HEAD_PAIR), F32),
                        pltpu.VMEM((2, SB_QROWS, 1), F32)],
        compiler_params=_params("parallel", "parallel", "arbitrary"),
        name="stick_breaking_attention",
    )(q, k, v, tri)


def _post_kernel(o_ref, x_ref, mod_ref, gain_ref, wo_ref, w1_ref, w2_ref, out_ref):
    gate1 = mod_ref[0, 2:3, :]
    shift2, scale2, gate2 = mod_ref[0, 3:4, :], mod_ref[0, 4:5, :], mod_ref[0, 5:6, :]
    x1 = x_ref[...] + gate1 * _dot(o_ref[...], wo_ref[...])
    h = _modulated_norm(x1, gain_ref[...], scale2, shift2).astype(BF16)
    acc = jnp.zeros_like(x1)
    for c in range(FFN_HIDDEN // FFN_SLAB):
        t = _dot(h, w1_ref[:, c * FFN_SLAB:(c + 1) * FFN_SLAB])
        t = jnp.square(jnp.maximum(t, 0.0)).astype(BF16)
        acc = acc + _dot(t, w2_ref[c * FFN_SLAB:(c + 1) * FFN_SLAB, :])
    out_ref[...] = x1 + gate2 * acc


def _post(o, x2, mod, gain, w_out, w1, w2, seq):
    tokens = x2.shape[0]
    grid, x_spec, mod_spec = _row_specs(tokens, seq, TM_PROJ)
    width = o.shape[1]
    out_spec, out_shape = _row_out(tokens, D_MODEL, TM_PROJ, F32)
    return pl.pallas_call(
        _post_kernel,
        grid=grid,
        in_specs=[pl.BlockSpec((TM_PROJ, width), lambda i: (i, 0)), x_spec, mod_spec,
                  _resident((1, D_MODEL)), _resident((width, D_MODEL)),
                  _resident((D_MODEL, FFN_HIDDEN)), _resident((FFN_HIDDEN, D_MODEL))],
        out_specs=out_spec, out_shape=out_shape,
        compiler_params=_params("parallel"),
        name="outproj_mlp",
    )(o, x2, mod, gain.reshape(1, D_MODEL), w_out.astype(BF16), w1.astype(BF16), w2.astype(BF16))


def kernel(x, c, ada_w, ada_b, norm_mix, norm_ffn, ffn_w1, ffn_w2, a_w_in, a_q_gain, a_k_gain, a_rel_bias, a_w_out, b_w_in, b_v_gain, b_w_s, b_b_s, b_w_out, c_w_in, c_w_gate_up, c_b_gate, c_o_gain, c_w_out, d_w_in, d_w_out):
    batch, seq, _ = x.shape
    x2 = x.reshape(batch * seq, D_MODEL)
    mods = _modulation(c, ada_w, ada_b)
    for i in range(DEPTH):
        m, j = i % N_MIXERS, i // N_MIXERS
        mod = mods[i]
        if m == 0:
            q, k, v = _inproj_a(x2, mod, norm_mix[i], a_w_in[j], a_q_gain[j], a_k_gain[j], seq)
            o = _attn_a(q, k, v, a_rel_bias[j], batch, seq)
            w_out = a_w_out[j]
        elif m == 1:
            o = _mixb(x2, mod, norm_mix[i], b_w_in[j], b_v_gain[j], b_w_s[j], b_b_s[j], seq)
            w_out = b_w_out[j]
        elif m == 2:
            q, k, v, r, la = _inproj_c(x2, mod, norm_mix[i], c_w_in[j], c_w_gate_up[j], c_b_gate[j], seq)
            o = _gla(q, k, v, r, la, c_o_gain[j], batch, seq)
            w_out = c_w_out[j]
        else:
            q, k, v = _inproj_d(x2, mod, norm_mix[i], d_w_in[j], seq)
            o = _stick_breaking(q, k, v, batch, seq)
            w_out = d_w_out[j]
        x2 = _post(o, x2, mod, norm_ffn[i], w_out, ffn_w1[i], ffn_w2[i], seq)
    return x2.reshape(batch, seq, D_MODEL)
```

```python
import functools

import jax
import jax.numpy as jnp
import numpy as np
from jax import lax
from jax.experimental import pallas as pl
from jax.experimental.pallas import tpu as pltpu

F32 = jnp.float32
BF16 = jnp.bfloat16

D_MODEL = 1024
DEPTH = 4
N_MIXERS = 4
CHUNK = 64
RMS_EPS = 1e-6

A_HEADS = 16
A_HEAD_DIM = 64
A_LEFT_CHUNKS = 8
A_MAX_REL = 256
A_BAND = (A_LEFT_CHUNKS + 1) * CHUNK
A_BAND_PAD = 640
A_GROUP = 8 * CHUNK

SG_FF = 6 * D_MODEL
SG_HALF = SG_FF // 2
SG_GROUPS = 8
SG_WINDOW = 128
SG_GROUP_WIDTH = SG_HALF // SG_GROUPS

GLA_HEADS = 4
GLA_KEY_WIDTH = D_MODEL // 2
GLA_DK = 128
GLA_DV = 256
GLA_GATE_RANK = 16
GLA_GATE_PAD = 128
GLA_TAU = 16.0
GLA_LEVELS = 6
GLA_STEP_CHUNKS = 8
GLA_UNROLL = 2

SB_HEADS = 16
SB_HEAD_DIM = 64
SB_QROWS = 512
SB_KBLOCK = 256
SB_STRIP = 64
SB_DEAD_LOG2 = 160.0
LOG2_E = float(np.log2(np.e))
SB_Q_SCALE = SB_HEAD_DIM ** -0.5 * LOG2_E
A_Q_SCALE = A_HEAD_DIM ** -0.5 * LOG2_E

FFN_HIDDEN = 4 * D_MODEL
FFN_SLAB = 1024

LANES = 128
HEAD_PAIR = LANES
VMEM_LIMIT_BYTES = 56 * 1024 * 1024

TM_PROJ = 512
TM_MIXB = 256

_NT = (((1,), (1,)), ((), ()))
_TN = (((0,), (0,)), ((), ()))


def _params(*semantics):
    return pltpu.CompilerParams(dimension_semantics=semantics,
                                vmem_limit_bytes=VMEM_LIMIT_BYTES)


def _dot(a, b):
    return jnp.dot(a, b, preferred_element_type=F32)


def _dot_nt(a, b):
    return lax.dot_general(a, b, _NT, preferred_element_type=F32)


def _resident(shape):
    zeros = (0,) * len(shape)
    return pl.BlockSpec(shape, lambda *_: zeros, pipeline_mode=pl.Buffered(1))


def _gelu(x):
    return 0.5 * x * (1.0 + lax.erf(x * (2.0 ** -0.5)))


def _modulated_norm(x, gain, scale, shift):
    ms = jnp.mean(x * x, axis=-1, keepdims=True)
    y = x * lax.rsqrt(ms + RMS_EPS) * gain
    return y * (1.0 + scale) + shift


def _mod_kernel(c_ref, w_ref, b_ref, o_ref):
    cond = jax.nn.silu(c_ref[...])
    o_ref[0] = jnp.dot(cond, w_ref[0], precision=lax.Precision.HIGHEST,
                       preferred_element_type=F32) + b_ref[0]


def _modulation(c, ada_w, ada_b):
    batch = c.shape[0]
    rows = 8
    c_pad = jnp.zeros((rows, D_MODEL), F32).at[:batch].set(c)
    tn = 1536
    out = pl.pallas_call(
        _mod_kernel,
        grid=(DEPTH, 6 * D_MODEL // tn),
        in_specs=[
            pl.BlockSpec((rows, D_MODEL), lambda i, j: (0, 0)),
            pl.BlockSpec((1, D_MODEL, tn), lambda i, j: (i, 0, j)),
            pl.BlockSpec((1, 1, tn), lambda i, j: (i, 0, j)),
        ],
        out_specs=pl.BlockSpec((1, rows, tn), lambda i, j: (i, 0, j)),
        out_shape=jax.ShapeDtypeStruct((DEPTH, rows, 6 * D_MODEL), F32),
        compiler_params=_params("parallel", "parallel"),
        name="adaln_modulation",
    )(c_pad, ada_w, ada_b.reshape(DEPTH, 1, 6 * D_MODEL))
    return out[:, :batch].reshape(DEPTH, batch, 6, D_MODEL)


def _normed_input(x_ref, mod_ref, gain_ref):
    shift, scale = mod_ref[0, 0:1, :], mod_ref[0, 1:2, :]
    return _modulated_norm(x_ref[...], gain_ref[...], scale, shift).astype(BF16)


def _inproj_a_kernel(x_ref, mod_ref, gain_ref, w_ref, qg_ref, kg_ref, seg_ref,
                     q_ref, k_ref, v_ref):
    h = _normed_input(x_ref, mod_ref, gain_ref)
    seg = seg_ref[...]

    def head_norm(y, gain):
        ssq = _dot((y * y).astype(BF16), seg)
        return y * lax.rsqrt(ssq * (1.0 / A_HEAD_DIM) + RMS_EPS) * gain

    width = seg.shape[0]
    for c in range(D_MODEL // width):
        cols = slice(c * width, (c + 1) * width)
        yq = _dot(h, w_ref[:, c * width:(c + 1) * width])
        q_ref[:, cols] = (head_norm(yq, qg_ref[:, cols]) * A_Q_SCALE).astype(BF16)
        yk = _dot(h, w_ref[:, D_MODEL + c * width:D_MODEL + (c + 1) * width])
        k_ref[:, cols] = head_norm(yk, kg_ref[:, cols]).astype(BF16)
        yv = _dot(h, w_ref[:, 2 * D_MODEL + c * width:2 * D_MODEL + (c + 1) * width])
        v_ref[:, cols] = yv.astype(BF16)


def _inproj_d_kernel(x_ref, mod_ref, gain_ref, w_ref, q_ref, k_ref, v_ref):
    h = _normed_input(x_ref, mod_ref, gain_ref)
    width = 512
    for c in range(D_MODEL // width):
        cols = slice(c * width, (c + 1) * width)
        yq = _dot(h, w_ref[:, c * width:(c + 1) * width])
        q_ref[:, cols] = (yq * SB_Q_SCALE).astype(BF16)
        yk = _dot(h, w_ref[:, D_MODEL + c * width:D_MODEL + (c + 1) * width])
        k_ref[:, cols] = yk.astype(BF16)
        yv = _dot(h, w_ref[:, 2 * D_MODEL + c * width:2 * D_MODEL + (c + 1) * width])
        v_ref[:, cols] = yv.astype(BF16)


def _inproj_c_kernel(x_ref, mod_ref, gain_ref, w_ref, wg_ref, bg_ref,
                     q_ref, k_ref, v_ref, r_ref, la_ref):
    h = _normed_input(x_ref, mod_ref, gain_ref)
    kw = GLA_KEY_WIDTH
    q_ref[...] = (_dot(h, w_ref[:, 0:kw]) * (GLA_DK ** -0.5)).astype(BF16)
    k_ref[...] = _dot(h, w_ref[:, kw:2 * kw]).astype(BF16)
    for c in range(2):
        cols = slice(c * kw, (c + 1) * kw)
        v_ref[:, cols] = _dot(h, w_ref[:, 2 * kw + c * kw:2 * kw + (c + 1) * kw]).astype(BF16)
        r_ref[:, cols] = _dot(h, w_ref[:, 4 * kw + c * kw:4 * kw + (c + 1) * kw]).astype(BF16)
    a = _dot(h, w_ref[:, 6 * kw:6 * kw + GLA_GATE_PAD]).astype(BF16)
    gate = _dot(a, wg_ref[...]) + bg_ref[...]
    la_ref[...] = jax.nn.log_sigmoid(gate) * (1.0 / GLA_TAU)


def _row_specs(tokens, seq, tm):
    tiles_per_batch = seq // tm
    x_spec = pl.BlockSpec((tm, D_MODEL), lambda i: (i, 0))
    mod_spec = pl.BlockSpec((1, 6, D_MODEL), lambda i: (i // tiles_per_batch, 0, 0))
    return (tokens // tm,), x_spec, mod_spec


def _row_out(tokens, width, tm, dtype=BF16):
    return (pl.BlockSpec((tm, width), lambda i: (i, 0)),
            jax.ShapeDtypeStruct((tokens, width), dtype))


def _inproj_a(x2, mod, gain, w_in, q_gain, k_gain, seq):
    tokens = x2.shape[0]
    grid, x_spec, mod_spec = _row_specs(tokens, seq, TM_PROJ)
    seg_width = 256
    seg_id = np.arange(seg_width) // A_HEAD_DIM
    seg = jnp.asarray(seg_id[:, None] == seg_id[None, :], BF16)
    specs, shapes = zip(*[_row_out(tokens, D_MODEL, TM_PROJ)] * 3)
    return pl.pallas_call(
        _inproj_a_kernel,
        grid=grid,
        in_specs=[x_spec, mod_spec, _resident((1, D_MODEL)), _resident((D_MODEL, 3 * D_MODEL)),
                  _resident((1, D_MODEL)), _resident((1, D_MODEL)),
                  _resident((seg_width, seg_width))],
        out_specs=list(specs), out_shape=list(shapes),
        compiler_params=_params("parallel"),
        name="inproj_relpos_attention",
    )(x2, mod, gain.reshape(1, D_MODEL), w_in.astype(BF16),
      jnp.tile(q_gain, A_HEADS).reshape(1, D_MODEL), jnp.tile(k_gain, A_HEADS).reshape(1, D_MODEL), seg)


def _inproj_d(x2, mod, gain, w_in, seq):
    tokens = x2.shape[0]
    grid, x_spec, mod_spec = _row_specs(tokens, seq, TM_PROJ)
    specs, shapes = zip(*[_row_out(tokens, D_MODEL, TM_PROJ)] * 3)
    return pl.pallas_call(
        _inproj_d_kernel,
        grid=grid,
        in_specs=[x_spec, mod_spec, _resident((1, D_MODEL)), _resident((D_MODEL, 3 * D_MODEL))],
        out_specs=list(specs), out_shape=list(shapes),
        compiler_params=_params("parallel"),
        name="inproj_stick_breaking",
    )(x2, mod, gain.reshape(1, D_MODEL), w_in.astype(BF16))


def _inproj_c(x2, mod, gain, w_in, w_gate_up, b_gate, seq):
    tokens = x2.shape[0]
    grid, x_spec, mod_spec = _row_specs(tokens, seq, TM_PROJ)
    main = 2 * GLA_KEY_WIDTH + 2 * D_MODEL
    w_pad = jnp.zeros((D_MODEL, main + GLA_GATE_PAD), BF16).at[:, :main + GLA_GATE_RANK].set(
        w_in.astype(BF16))
    wg_pad = jnp.zeros((GLA_GATE_PAD, GLA_KEY_WIDTH), BF16).at[:GLA_GATE_RANK].set(
        w_gate_up.astype(BF16))
    outs = [_row_out(tokens, GLA_KEY_WIDTH, TM_PROJ), _row_out(tokens, GLA_KEY_WIDTH, TM_PROJ),
            _row_out(tokens, D_MODEL, TM_PROJ), _row_out(tokens, D_MODEL, TM_PROJ),
            _row_out(tokens, GLA_KEY_WIDTH, TM_PROJ, F32)]
    specs, shapes = zip(*outs)
    return pl.pallas_call(
        _inproj_c_kernel,
        grid=grid,
        in_specs=[x_spec, mod_spec, _resident((1, D_MODEL)),
                  _resident((D_MODEL, main + GLA_GATE_PAD)),
                  _resident((GLA_GATE_PAD, GLA_KEY_WIDTH)), _resident((1, GLA_KEY_WIDTH))],
        out_specs=list(specs), out_shape=list(shapes),
        compiler_params=_params("parallel"),
        name="inproj_gla",
    )(x2, mod, gain.reshape(1, D_MODEL), w_pad, wg_pad, b_gate.reshape(1, GLA_KEY_WIDTH))


def _attn_a_kernel(q_ref, kp_ref, kc_ref, vp_ref, vc_ref, bias_ref, o_ref, kwin, vwin, bias_scr):
    block = pl.program_id(2)
    kwin[0:A_GROUP, :] = kp_ref[...]
    kwin[A_GROUP:2 * A_GROUP, :] = kc_ref[...]
    vwin[0:A_GROUP, :] = vp_ref[...]
    vwin[A_GROUP:2 * A_GROUP, :] = vc_ref[...]
    tail = jnp.zeros((A_BAND_PAD - A_BAND, HEAD_PAIR), BF16)
    kwin[2 * A_GROUP:, :] = tail
    vwin[2 * A_GROUP:, :] = tail

    lane = lax.broadcasted_iota(jnp.int32, (CHUNK, HEAD_PAIR), 1)
    low_half = lane < A_HEAD_DIM
    chunks = A_GROUP // CHUNK

    @pl.when(block == 0)
    def _():
        col = lax.broadcasted_iota(jnp.int32, (2 * CHUNK, A_BAND_PAD), 1)
        for c in range(chunks):
            bias_scr[c] = jnp.where(col >= A_GROUP - c * CHUNK, bias_ref[0], -jnp.inf)

    @pl.when(block == 1)
    def _():
        for c in range(chunks):
            bias_scr[c] = bias_ref[0]

    scores = []
    for c in range(chunks):
        qc = q_ref[c * CHUNK:(c + 1) * CHUNK, :]
        q2 = jnp.concatenate([jnp.where(low_half, qc, 0), jnp.where(low_half, 0, qc)], axis=0)
        scores.append(_dot_nt(q2, kwin[c * CHUNK:c * CHUNK + A_BAND_PAD, :]) + bias_scr[c])
    maxes = [jnp.max(s, axis=-1, keepdims=True) for s in scores]
    probs, denoms = [], []
    for s, m in zip(scores, maxes):
        p = jnp.exp2(s - m)
        denoms.append(jnp.sum(p, axis=-1, keepdims=True))
        probs.append(p.astype(BF16))
    outs = [_dot(p, vwin[c * CHUNK:c * CHUNK + A_BAND_PAD, :]) / d
            for c, (p, d) in enumerate(zip(probs, denoms))]
    for c in range(chunks):
        o_ref[c * CHUNK:(c + 1) * CHUNK, :] = jnp.where(
            low_half, outs[c][0:CHUNK], outs[c][CHUNK:2 * CHUNK]).astype(BF16)


def _relpos_bias_table(rel_bias):
    rb = rel_bias.astype(F32) * LOG2_E
    far = A_MAX_REL + CHUNK - 1
    ext = jnp.concatenate([jnp.broadcast_to(rb[:, far:far + 1], (A_HEADS, far + 1)),
                           rb[:, far - 1::-1]], axis=1)
    table = jnp.stack([ext[:, CHUNK - 1 - i:CHUNK - 1 - i + A_BAND] for i in range(CHUNK)], axis=1)
    table = jnp.pad(table, ((0, 0), (0, 0), (0, A_BAND_PAD - A_BAND)), constant_values=-jnp.inf)
    return table.reshape(A_HEADS // 2, 2 * CHUNK, A_BAND_PAD)


def _attn_a(q, k, v, rel_bias, batch, seq):
    tokens = q.shape[0]
    blocks = seq // A_GROUP
    pairs = D_MODEL // HEAD_PAIR
    cur = lambda b, h, j: (b * blocks + j, h)
    prev = lambda b, h, j: (b * blocks + jnp.maximum(j - 1, 0), h)
    blk = (A_GROUP, HEAD_PAIR)
    win_rows = 2 * A_GROUP + A_BAND_PAD - A_BAND
    return pl.pallas_call(
        _attn_a_kernel,
        grid=(batch, pairs, blocks),
        in_specs=[pl.BlockSpec(blk, cur), pl.BlockSpec(blk, prev), pl.BlockSpec(blk, cur),
                  pl.BlockSpec(blk, prev), pl.BlockSpec(blk, cur),
                  pl.BlockSpec((1, 2 * CHUNK, A_BAND_PAD), lambda b, h, j: (h, 0, 0))],
        out_specs=pl.BlockSpec(blk, cur),
        out_shape=jax.ShapeDtypeStruct((tokens, D_MODEL), BF16),
        scratch_shapes=[pltpu.VMEM((win_rows, HEAD_PAIR), BF16),
                        pltpu.VMEM((win_rows, HEAD_PAIR), BF16),
                        pltpu.VMEM((A_GROUP // CHUNK, 2 * CHUNK, A_BAND_PAD), F32)],
        compiler_params=_params("parallel", "parallel", "arbitrary"),
        name="relpos_band_attention",
    )(q, k, k, v, v, _relpos_bias_table(rel_bias))


def _mixb_kernel(x_ref, mod_ref, gain_ref, w_ref, vg_ref, ws_ref, bs_ref, o_ref, v_scr):
    h = _normed_input(x_ref, mod_ref, gain_ref)
    tm = h.shape[0]
    slab = 512
    ssq = jnp.zeros((tm, 1), F32)
    for c in range(SG_HALF // slab):
        zc = _gelu(_dot(h, w_ref[:, SG_HALF + c * slab:SG_HALF + (c + 1) * slab]))
        ssq = ssq + jnp.sum(zc * zc, axis=-1, keepdims=True)
        v_scr[:, c * slab:(c + 1) * slab] = zc
    inv_rms = lax.rsqrt(ssq * (1.0 / SG_HALF) + RMS_EPS)

    row = lax.broadcasted_iota(jnp.int32, (SG_WINDOW, SG_WINDOW), 0) // CHUNK
    colc = lax.broadcasted_iota(jnp.int32, (SG_WINDOW, SG_WINDOW), 1) // CHUNK
    block_lower = row >= colc
    gw = SG_GROUP_WIDTH
    for g in range(SG_GROUPS):
        cols = slice(g * gw, (g + 1) * gw)
        w_s = jnp.where(block_lower, ws_ref[g], 0.0).astype(BF16)
        vn = (v_scr[:, cols] * inv_rms * vg_ref[:, cols]).astype(BF16)
        u = _gelu(_dot(h, w_ref[:, g * gw:(g + 1) * gw]))
        for w in range(tm // SG_WINDOW):
            rows = slice(w * SG_WINDOW, (w + 1) * SG_WINDOW)
            vm = _dot(w_s, vn[rows]) + bs_ref[g]
            o_ref[rows, cols] = (u[rows] * vm).astype(BF16)


def _mixb(x2, mod, gain, w_in, v_gain, w_s, b_s, seq):
    tokens = x2.shape[0]
    grid, x_spec, mod_spec = _row_specs(tokens, seq, TM_MIXB)
    bias = jnp.broadcast_to(b_s.astype(F32)[:, :, None], (SG_GROUPS, SG_WINDOW, SG_GROUP_WIDTH))
    out_spec, out_shape = _row_out(tokens, SG_HALF, TM_MIXB)
    return pl.pallas_call(
        _mixb_kernel,
        grid=grid,
        in_specs=[x_spec, mod_spec, _resident((1, D_MODEL)), _resident((D_MODEL, SG_FF)),
                  _resident((1, SG_HALF)), _resident((SG_GROUPS, SG_WINDOW, SG_WINDOW)),
                  _resident((SG_GROUPS, SG_WINDOW, SG_GROUP_WIDTH))],
        out_specs=out_spec, out_shape=out_shape,
        scratch_shapes=[pltpu.VMEM((TM_MIXB, SG_HALF), F32)],
        compiler_params=_params("parallel"),
        name="gmlp_spatial_gating",
    )(x2, mod, gain.reshape(1, D_MODEL), w_in.astype(BF16), v_gain.reshape(1, SG_HALF),
      w_s.astype(F32), bias)


def _gla_decay_matrix():
    r = np.arange(CHUNK)[:, None]
    i = np.arange(CHUNK)[None, :]
    parts = [i <= r, i > r]
    for level in range(GLA_LEVELS):
        n = (CHUNK // 2) >> level
        ref = (r // (2 * n)) * 2 * n + n
        upper = (r % (2 * n)) >= n
        parts.append(np.where(upper, (i > ref) & (i <= r), (i > r) & (i <= ref)))
    m = np.concatenate(parts, axis=0)
    return jnp.asarray(np.concatenate([m, m], axis=1), BF16)


def _gla_level_map():
    t = np.arange(CHUNK)[:, None]
    s = np.arange(CHUNK)[None, :]
    out = np.full((CHUNK, CHUNK), -1, np.int32)
    out[t == s] = 0
    for level in range(GLA_LEVELS):
        n = (CHUNK // 2) >> level
        hit = (t // (2 * n) == s // (2 * n)) & (t % (2 * n) >= n) & (s % (2 * n) < n)
        out[hit] = 1 + level
    return jnp.asarray(out)


def _gla_kernel(q_ref, k_ref, v_ref, r_ref, la_ref, dm_ref, lvl_ref, og_ref, o_ref, state):
    @pl.when(pl.program_id(1) == 0)
    def _():
        state[...] = jnp.zeros_like(state)

    dm = dm_ref[...]
    lvl = lvl_ref[...]

    heads = range(GLA_HEADS)
    kcols = [slice(h * GLA_DK, (h + 1) * GLA_DK) for h in heads]
    vcols = [slice(h * GLA_DV, (h + 1) * GLA_DV) for h in heads]

    def group(g, carry):
        base = g * (GLA_UNROLL * CHUNK)
        chunk_rows = [pl.ds(pl.multiple_of(base + u * CHUNK, CHUNK), CHUNK) for u in range(GLA_UNROLL)]
        decays = []
        for rows in chunk_rows:
            la = la_ref[rows, :]
            la_hi = la.astype(BF16)
            la_lo = (la - la_hi.astype(F32)).astype(BF16)
            decays.append(jnp.exp(_dot(dm, jnp.concatenate([la_hi, la_lo], axis=0))))
        units = [(u, h) for u in range(GLA_UNROLL) for h in heads]
        qs = [q_ref[chunk_rows[u], kcols[h]].astype(F32) for u, h in units]
        ks = [k_ref[chunk_rows[u], kcols[h]].astype(F32) for u, h in units]
        pairs = []
        for (u, h), qh, kh in zip(units, qs, ks):
            level_pairs = [_dot_nt(qh.astype(BF16), kh.astype(BF16))]
            for level in range(GLA_LEVELS):
                e = decays[u][(2 + level) * CHUNK:(3 + level) * CHUNK, kcols[h]]
                level_pairs.append(_dot_nt((qh * e).astype(BF16), (kh * e).astype(BF16)))
            pairs.append(level_pairs)
        intra = []
        for (u, h), level_pairs in zip(units, pairs):
            att = jnp.where(lvl == 0, level_pairs[0], 0.0)
            for level in range(GLA_LEVELS):
                att = jnp.where(lvl == 1 + level, level_pairs[1 + level], att)
            intra.append(_dot(att.astype(BF16), v_ref[chunk_rows[u], vcols[h]]))
        states = [state[h] for h in heads]
        for idx, (u, h) in enumerate(units):
            rows, kc, vc = chunk_rows[u], kcols[h], vcols[h]
            decay = decays[u]
            o = intra[idx] + _dot_nt((qs[idx] * decay[0:CHUNK, kc]).astype(BF16), states[h].astype(BF16))
            k_dec = (ks[idx] * decay[CHUNK:2 * CHUNK, kc]).astype(BF16)
            states[h] = (states[h] * decay[CHUNK - 1:CHUNK, kc]
                         + lax.dot_general(v_ref[rows, vc], k_dec, _TN, preferred_element_type=F32))
            ms = jnp.mean(o * o, axis=-1, keepdims=True)
            on = o * lax.rsqrt(ms + RMS_EPS) * og_ref[:, vc]
            o_ref[rows, vc] = (jax.nn.silu(r_ref[rows, vc].astype(F32)) * on).astype(BF16)
        for h in heads:
            state[h] = states[h]
        return carry

    lax.fori_loop(0, GLA_STEP_CHUNKS // GLA_UNROLL, group, 0)


def _gla(q, k, v, r, la, o_gain, batch, seq):
    tokens = q.shape[0]
    step = GLA_STEP_CHUNKS * CHUNK
    blocks = seq // step
    idx = lambda b, j: (b * blocks + j, 0)
    return pl.pallas_call(
        _gla_kernel,
        grid=(batch, blocks),
        in_specs=[pl.BlockSpec((step, GLA_KEY_WIDTH), idx), pl.BlockSpec((step, GLA_KEY_WIDTH), idx),
                  pl.BlockSpec((step, D_MODEL), idx), pl.BlockSpec((step, D_MODEL), idx),
                  pl.BlockSpec((step, GLA_KEY_WIDTH), idx),
                  _resident((8 * CHUNK, 2 * CHUNK)), _resident((CHUNK, CHUNK)), _resident((1, D_MODEL))],
        out_specs=pl.BlockSpec((step, D_MODEL), idx),
        out_shape=jax.ShapeDtypeStruct((tokens, D_MODEL), BF16),
        scratch_shapes=[pltpu.VMEM((GLA_HEADS, GLA_DV, GLA_DK), F32)],
        compiler_params=_params("parallel", "arbitrary"),
        name="gla_scan",
    )(q, k, v, r, la, _gla_decay_matrix(), _gla_level_map(),
      jnp.tile(o_gain, GLA_HEADS).reshape(1, D_MODEL))


def _sb_kernel(q_ref, k_ref, v_ref, tri_ref, o_ref, z_scr, w_scr, acc_scr, later_scr):
    qb = pl.program_id(2)
    diag_visits = SB_QROWS // SB_KBLOCK
    visits = diag_visits * (qb + 1)
    last_block = k_ref.shape[0] // SB_KBLOCK - 1
    tri = tri_ref[...]
    lane = lax.broadcasted_iota(jnp.int32, (SB_QROWS, HEAD_PAIR), 1)
    low_half = lane < SB_HEAD_DIM
    q = q_ref[...]
    q_heads = (jnp.where(low_half, q, 0), jnp.where(low_half, 0, q))
    sign_bit = jnp.uint32(0x80000000)
    inv_ln2 = 1.0 / np.log(2.0)
    w_scr[...] = jnp.zeros_like(w_scr)
    acc_scr[...] = jnp.zeros_like(acc_scr)
    later_scr[...] = jnp.zeros_like(later_scr)

    def key_rows(kb):
        return pl.ds(pl.multiple_of(kb * SB_KBLOCK, SB_KBLOCK), SB_KBLOCK)

    def scores(kb, slot, first_row=0):
        kblk = k_ref[key_rows(kb), :]
        for h in range(2):
            z_scr[slot, h, first_row:, :] = _dot_nt(q_heads[h][first_row:], kblk)

    def visit(v, slot, masked, first_row=0):
        kb = visits - 1 - v
        scores(jnp.maximum(kb - 1, 0), 1 - slot)
        vprev = v_ref[key_rows(jnp.minimum(kb + 1, last_block)), :]
        for h in range(2):
            acc_scr[h] += _dot(w_scr[h], vprev)
        if masked:
            t_idx = lax.broadcasted_iota(jnp.int32, (SB_STRIP, SB_KBLOCK), 0) + qb * SB_QROWS
            s_idx = lax.broadcasted_iota(jnp.int32, (SB_STRIP, SB_KBLOCK), 1) + kb * SB_KBLOCK
        strips = [slice(r, r + SB_STRIP) for r in range(first_row, SB_QROWS, SB_STRIP)]
        incls = []
        for h in range(2):
            sps = []
            for rows in strips:
                zr = z_scr[slot, h, rows, :]
                neg_abs = lax.bitcast_convert_type(lax.bitcast_convert_type(zr, jnp.uint32) | sign_bit, F32)
                sp = jnp.maximum(zr, 0.0) + jnp.log(1.0 + jnp.exp2(neg_abs)) * inv_ln2
                if masked:
                    sp = jnp.where(s_idx < t_idx + rows.start, sp, 0.0)
                sps.append(sp.astype(BF16))
            incls.append(_dot(jnp.concatenate(sps, axis=0), tri))
        for h in range(2):
            for rows in strips:
                incl = incls[h][rows.start - first_row:rows.stop - first_row]
                w = jnp.exp2(z_scr[slot, h, rows, :] - incl - later_scr[h, rows, :])
                if masked:
                    w = jnp.where(s_idx < t_idx + rows.start, w, 0.0)
                w_scr[h, rows, :] = w.astype(BF16)
                later_scr[h, rows, :] += incl[:, 0:1]

    def finished(v):
        return jnp.logical_or(v >= visits, jnp.min(later_scr[...]) >= SB_DEAD_LOG2).astype(jnp.int32)

    scores(visits - 1, 0, first_row=SB_QROWS - SB_KBLOCK)
    visit(0, 0, True, first_row=SB_QROWS - SB_KBLOCK)
    visit(1, 1, True)

    def step(carry):
        v, _ = carry
        visit(v, 0, False)
        visit(v + 1, 1, False)
        return v + 2, finished(v + 2)

    assert diag_visits == 2
    v_end, _ = lax.while_loop(lambda carry: carry[1] == 0, step,
                              (jnp.int32(diag_visits), (visits <= diag_visits).astype(jnp.int32)))
    pending = v_ref[key_rows(visits - v_end), :]
    out = [acc_scr[h] + _dot(w_scr[h], pending) for h in range(2)]
    o_ref[...] = jnp.where(low_half, out[0], out[1]).astype(BF16)


def _stick_breaking(q, k, v, batch, seq):
    tokens = q.shape[0]
    blocks = seq // SB_QROWS
    pairs = D_MODEL // HEAD_PAIR
    j = np.arange(SB_KBLOCK)
    tri = jnp.asarray(j[:, None] >= j[None, :], BF16)
    qo_spec = pl.BlockSpec((SB_QROWS, HEAD_PAIR), lambda b, h, i: (b * blocks + i, h))
    kv_spec = pl.BlockSpec((seq, HEAD_PAIR), lambda b, h, i: (b, h))
    return pl.pallas_call(
        _sb_kernel,
        grid=(batch, pairs, blocks),
        in_specs=[qo_spec, kv_spec, kv_spec, _resident((SB_KBLOCK, SB_KBLOCK))],
        out_specs=qo_spec,
        out_shape=jax.ShapeDtypeStruct((tokens, D_MODEL), BF16),
        scratch_shapes=[pltpu.VMEM((2, 2, SB_QROWS, SB_KBLOCK), F32),
                        pltpu.VMEM((2, SB_QROWS, SB_KBLOCK), BF16),
                        pltpu.VMEM((2, SB_QROWS, HEAD_PAIR), F32),
                        pltpu.VMEM((2, SB_QROWS, 1), F32)],
        compiler_params=_params("parallel", "parallel", "arbitrary"),
        name="stick_breaking_attention",
    )(q, k, v, tri)


def _post_kernel(o_ref, x_ref, mod_ref, gain_ref, wo_ref, w1_ref, w2_ref, out_ref):
    gate1 = mod_ref[0, 2:3, :]
    shift2, scale2, gate2 = mod_ref[0, 3:4, :], mod_ref[0, 4:5, :], mod_ref[0, 5:6, :]
    x1 = x_ref[...] + gate1 * _dot(o_ref[...], wo_ref[...])
    h = _modulated_norm(x1, gain_ref[...], scale2, shift2).astype(BF16)
    acc = jnp.zeros_like(x1)
    for c in range(FFN_HIDDEN // FFN_SLAB):
        t = _dot(h, w1_ref[:, c * FFN_SLAB:(c + 1) * FFN_SLAB])
        t = jnp.square(jnp.maximum(t, 0.0)).astype(BF16)
        acc = acc + _dot(t, w2_ref[c * FFN_SLAB:(c + 1) * FFN_SLAB, :])
    out_ref[...] = x1 + gate2 * acc


def _post(o, x2, mod, gain, w_out, w1, w2, seq):
    tokens = x2.shape[0]
    grid, x_spec, mod_spec = _row_specs(tokens, seq, TM_PROJ)
    width = o.shape[1]
    out_spec, out_shape = _row_out(tokens, D_MODEL, TM_PROJ, F32)
    return pl.pallas_call(
        _post_kernel,
        grid=grid,
        in_specs=[pl.BlockSpec((TM_PROJ, width), lambda i: (i, 0)), x_spec, mod_spec,
                  _resident((1, D_MODEL)), _resident((width, D_MODEL)),
                  _resident((D_MODEL, FFN_HIDDEN)), _resident((FFN_HIDDEN, D_MODEL))],
        out_specs=out_spec, out_shape=out_shape,
        compiler_params=_params("parallel"),
        name="outproj_mlp",
    )(o, x2, mod, gain.reshape(1, D_MODEL), w_out.astype(BF16), w1.astype(BF16), w2.astype(BF16))


def kernel(x, c, ada_w, ada_b, norm_mix, norm_ffn, ffn_w1, ffn_w2, a_w_in, a_q_gain, a_k_gain, a_rel_bias, a_w_out, b_w_in, b_v_gain, b_w_s, b_b_s, b_w_out, c_w_in, c_w_gate_up, c_b_gate, c_o_gain, c_w_out, d_w_in, d_w_out):
    batch, seq, _ = x.shape
    x2 = x.reshape(batch * seq, D_MODEL)
    mods = _modulation(c, ada_w, ada_b)
    for i in range(DEPTH):
        m, j = i % N_MIXERS, i // N_MIXERS
        mod = mods[i]
        if m == 0:
            q, k, v = _inproj_a(x2, mod, norm_mix[i], a_w_in[j], a_q_gain[j], a_k_gain[j], seq)
            o = _attn_a(q, k, v, a_rel_bias[j], batch, seq)
            w_out = a_w_out[j]
        elif m == 1:
            o = _mixb(x2, mod, norm_mix[i], b_w_in[j], b_v_gain[j], b_w_s[j], b_b_s[j], seq)
            w_out = b_w_out[j]
        elif m == 2:
            q, k, v, r, la = _inproj_c(x2, mod, norm_mix[i], c_w_in[j], c_w_gate_up[j], c_b_gate[j], seq)
            o = _gla(q, k, v, r, la, c_o_gain[j], batch, seq)
            w_out = c_w_out[j]
        else:
            q, k, v = _inproj_d(x2, mod, norm_mix[i], d_w_in[j], seq)
            o = _stick_breaking(q, k, v, batch, seq)
            w_out = d_w_out[j]
        x2 = _post(o, x2, mod, norm_ffn[i], w_out, ffn_w1[i], ffn_w2[i], seq)
    return x2.reshape(batch, seq, D_MODEL)
```

```python
import functools

import jax
import jax.numpy as jnp
import numpy as np
from jax import lax
from jax.experimental import pallas as pl
from jax.experimental.pallas import tpu as pltpu

F32 = jnp.float32
BF16 = jnp.bfloat16

D_MODEL = 1024
DEPTH = 4
N_MIXERS = 4
CHUNK = 64
RMS_EPS = 1e-6

A_HEADS = 16
A_HEAD_DIM = 64
A_LEFT_CHUNKS = 8
A_MAX_REL = 256
A_BAND = (A_LEFT_CHUNKS + 1) * CHUNK
A_BAND_PAD = 640
A_GROUP = 8 * CHUNK
A_INTERLEAVE = 4

SG_FF = 6 * D_MODEL
SG_HALF = SG_FF // 2
SG_GROUPS = 8
SG_WINDOW = 128
SG_GROUP_WIDTH = SG_HALF // SG_GROUPS

GLA_HEADS = 4
GLA_KEY_WIDTH = D_MODEL // 2
GLA_DK = 128
GLA_DV = 256
GLA_GATE_RANK = 16
GLA_GATE_PAD = 128
GLA_TAU = 16.0
GLA_LEVELS = 6
GLA_STEP_CHUNKS = 8
GLA_UNROLL = 4

SB_HEADS = 16
SB_HEAD_DIM = 64
SB_QROWS = 512
SB_KBLOCK = 256
SB_STRIP = 64
SB_DEAD_LOG2 = 160.0
LOG2_E = float(np.log2(np.e))
SB_Q_SCALE = SB_HEAD_DIM ** -0.5 * LOG2_E
A_Q_SCALE = A_HEAD_DIM ** -0.5 * LOG2_E

FFN_HIDDEN = 4 * D_MODEL
FFN_SLAB = 1024

LANES = 128
HEAD_PAIR = LANES
VMEM_LIMIT_BYTES = 56 * 1024 * 1024

TM_PROJ = 512
TM_MIXB = 256

_NT = (((1,), (1,)), ((), ()))
_TN = (((0,), (0,)), ((), ()))


def _params(*semantics):
    return pltpu.CompilerParams(dimension_semantics=semantics,
                                vmem_limit_bytes=VMEM_LIMIT_BYTES)


def _dot(a, b):
    return jnp.dot(a, b, preferred_element_type=F32)


def _dot_nt(a, b):
    return lax.dot_general(a, b, _NT, preferred_element_type=F32)


def _resident(shape):
    zeros = (0,) * len(shape)
    return pl.BlockSpec(shape, lambda *_: zeros, pipeline_mode=pl.Buffered(1))


def _gelu(x):
    return 0.5 * x * (1.0 + lax.erf(x * (2.0 ** -0.5)))


def _modulated_norm(x, gain, scale, shift):
    ms = jnp.mean(x * x, axis=-1, keepdims=True)
    y = x * lax.rsqrt(ms + RMS_EPS) * gain
    return y * (1.0 + scale) + shift


def _mod_kernel(c_ref, w_ref, b_ref, o_ref):
    cond = jax.nn.silu(c_ref[...])
    o_ref[0] = jnp.dot(cond, w_ref[0], precision=lax.Precision.HIGHEST,
                       preferred_element_type=F32) + b_ref[0]


def _modulation(c, ada_w, ada_b):
    batch = c.shape[0]
    rows = 8
    c_pad = jnp.zeros((rows, D_MODEL), F32).at[:batch].set(c)
    tn = 1536
    out = pl.pallas_call(
        _mod_kernel,
        grid=(DEPTH, 6 * D_MODEL // tn),
        in_specs=[
            pl.BlockSpec((rows, D_MODEL), lambda i, j: (0, 0)),
            pl.BlockSpec((1, D_MODEL, tn), lambda i, j: (i, 0, j)),
            pl.BlockSpec((1, 1, tn), lambda i, j: (i, 0, j)),
        ],
        out_specs=pl.BlockSpec((1, rows, tn), lambda i, j: (i, 0, j)),
        out_shape=jax.ShapeDtypeStruct((DEPTH, rows, 6 * D_MODEL), F32),
        compiler_params=_params("parallel", "parallel"),
        name="adaln_modulation",
    )(c_pad, ada_w, ada_b.reshape(DEPTH, 1, 6 * D_MODEL))
    return out[:, :batch].reshape(DEPTH, batch, 6, D_MODEL)


def _normed_input(x_ref, mod_ref, gain_ref):
    shift, scale = mod_ref[0, 0:1, :], mod_ref[0, 1:2, :]
    return _modulated_norm(x_ref[...], gain_ref[...], scale, shift).astype(BF16)


def _inproj_a_kernel(x_ref, mod_ref, gain_ref, w_ref, qg_ref, kg_ref, seg_ref,
                     q_ref, k_ref, v_ref):
    h = _normed_input(x_ref, mod_ref, gain_ref)
    seg = seg_ref[...]

    def head_norm(y, gain):
        ssq = _dot((y * y).astype(BF16), seg)
        return y * lax.rsqrt(ssq * (1.0 / A_HEAD_DIM) + RMS_EPS) * gain

    width = seg.shape[0]
    slabs = [(part, c) for c in range(D_MODEL // width) for part in range(3)]

    def project(part, c):
        start = part * D_MODEL + c * width
        return _dot(h, w_ref[:, start:start + width])

    y_next = project(*slabs[0])
    for i, (part, c) in enumerate(slabs):
        y = y_next
        if i + 1 < len(slabs):
            y_next = project(*slabs[i + 1])
        cols = slice(c * width, (c + 1) * width)
        if part == 0:
            q_ref[:, cols] = (head_norm(y, qg_ref[:, cols]) * A_Q_SCALE).astype(BF16)
        elif part == 1:
            k_ref[:, cols] = head_norm(y, kg_ref[:, cols]).astype(BF16)
        else:
            v_ref[:, cols] = y.astype(BF16)


def _inproj_d_kernel(x_ref, mod_ref, gain_ref, w_ref, q_ref, k_ref, v_ref):
    h = _normed_input(x_ref, mod_ref, gain_ref)
    width = 512
    for c in range(D_MODEL // width):
        cols = slice(c * width, (c + 1) * width)
        yq = _dot(h, w_ref[:, c * width:(c + 1) * width])
        q_ref[:, cols] = (yq * SB_Q_SCALE).astype(BF16)
        yk = _dot(h, w_ref[:, D_MODEL + c * width:D_MODEL + (c + 1) * width])
        k_ref[:, cols] = yk.astype(BF16)
        yv = _dot(h, w_ref[:, 2 * D_MODEL + c * width:2 * D_MODEL + (c + 1) * width])
        v_ref[:, cols] = yv.astype(BF16)


def _inproj_c_kernel(x_ref, mod_ref, gain_ref, w_ref, wg_ref, bg_ref,
                     q_ref, k_ref, v_ref, r_ref, la_ref):
    h = _normed_input(x_ref, mod_ref, gain_ref)
    kw = GLA_KEY_WIDTH
    a = _dot(h, w_ref[:, 6 * kw:6 * kw + GLA_GATE_PAD]).astype(BF16)
    y_next = _dot(h, w_ref[:, 0:kw])
    gate = _dot(a, wg_ref[...]) + bg_ref[...]
    la_ref[...] = jax.nn.log_sigmoid(gate) * (1.0 / GLA_TAU)
    outs = [(q_ref, slice(0, kw)), (k_ref, slice(0, kw)), (v_ref, slice(0, kw)), (v_ref, slice(kw, 2 * kw)),
            (r_ref, slice(0, kw)), (r_ref, slice(kw, 2 * kw))]
    for i, (ref, cols) in enumerate(outs):
        y = y_next
        if i + 1 < len(outs):
            y_next = _dot(h, w_ref[:, (i + 1) * kw:(i + 2) * kw])
        if i == 0:
            y = y * (GLA_DK ** -0.5)
        ref[:, cols] = y.astype(BF16)


def _row_specs(tokens, seq, tm):
    tiles_per_batch = seq // tm
    x_spec = pl.BlockSpec((tm, D_MODEL), lambda i: (i, 0))
    mod_spec = pl.BlockSpec((1, 6, D_MODEL), lambda i: (i // tiles_per_batch, 0, 0))
    return (tokens // tm,), x_spec, mod_spec


def _row_out(tokens, width, tm, dtype=BF16):
    return (pl.BlockSpec((tm, width), lambda i: (i, 0)),
            jax.ShapeDtypeStruct((tokens, width), dtype))


def _inproj_a(x2, mod, gain, w_in, q_gain, k_gain, seq):
    tokens = x2.shape[0]
    grid, x_spec, mod_spec = _row_specs(tokens, seq, TM_PROJ)
    seg_width = 256
    seg_id = np.arange(seg_width) // A_HEAD_DIM
    seg = jnp.asarray(seg_id[:, None] == seg_id[None, :], BF16)
    specs, shapes = zip(*[_row_out(tokens, D_MODEL, TM_PROJ)] * 3)
    return pl.pallas_call(
        _inproj_a_kernel,
        grid=grid,
        in_specs=[x_spec, mod_spec, _resident((1, D_MODEL)), _resident((D_MODEL, 3 * D_MODEL)),
                  _resident((1, D_MODEL)), _resident((1, D_MODEL)),
                  _resident((seg_width, seg_width))],
        out_specs=list(specs), out_shape=list(shapes),
        compiler_params=_params("parallel"),
        name="inproj_relpos_attention",
    )(x2, mod, gain.reshape(1, D_MODEL), w_in.astype(BF16),
      jnp.tile(q_gain, A_HEADS).reshape(1, D_MODEL), jnp.tile(k_gain, A_HEADS).reshape(1, D_MODEL), seg)


def _inproj_d(x2, mod, gain, w_in, seq):
    tokens = x2.shape[0]
    grid, x_spec, mod_spec = _row_specs(tokens, seq, TM_PROJ)
    specs, shapes = zip(*[_row_out(tokens, D_MODEL, TM_PROJ)] * 3)
    return pl.pallas_call(
        _inproj_d_kernel,
        grid=grid,
        in_specs=[x_spec, mod_spec, _resident((1, D_MODEL)), _resident((D_MODEL, 3 * D_MODEL))],
        out_specs=list(specs), out_shape=list(shapes),
        compiler_params=_params("parallel"),
        name="inproj_stick_breaking",
    )(x2, mod, gain.reshape(1, D_MODEL), w_in.astype(BF16))


def _inproj_c(x2, mod, gain, w_in, w_gate_up, b_gate, seq):
    tokens = x2.shape[0]
    grid, x_spec, mod_spec = _row_specs(tokens, seq, TM_PROJ)
    main = 2 * GLA_KEY_WIDTH + 2 * D_MODEL
    w_pad = jnp.zeros((D_MODEL, main + GLA_GATE_PAD), BF16).at[:, :main + GLA_GATE_RANK].set(
        w_in.astype(BF16))
    wg_pad = jnp.zeros((GLA_GATE_PAD, GLA_KEY_WIDTH), BF16).at[:GLA_GATE_RANK].set(
        w_gate_up.astype(BF16))
    outs = [_row_out(tokens, GLA_KEY_WIDTH, TM_PROJ), _row_out(tokens, GLA_KEY_WIDTH, TM_PROJ),
            _row_out(tokens, D_MODEL, TM_PROJ), _row_out(tokens, D_MODEL, TM_PROJ),
            _row_out(tokens, GLA_KEY_WIDTH, TM_PROJ, F32)]
    specs, shapes = zip(*outs)
    return pl.pallas_call(
        _inproj_c_kernel,
        grid=grid,
        in_specs=[x_spec, mod_spec, _resident((1, D_MODEL)),
                  _resident((D_MODEL, main + GLA_GATE_PAD)),
                  _resident((GLA_GATE_PAD, GLA_KEY_WIDTH)), _resident((1, GLA_KEY_WIDTH))],
        out_specs=list(specs), out_shape=list(shapes),
        compiler_params=_params("parallel"),
        name="inproj_gla",
    )(x2, mod, gain.reshape(1, D_MODEL), w_pad, wg_pad, b_gate.reshape(1, GLA_KEY_WIDTH))


def _attn_a_kernel(q_ref, kp_ref, kc_ref, vp_ref, vc_ref, bias_ref, o_ref, kwin, vwin, bias_scr):
    block = pl.program_id(2)
    kwin[0:A_GROUP, :] = kp_ref[...]
    kwin[A_GROUP:2 * A_GROUP, :] = kc_ref[...]
    vwin[0:A_GROUP, :] = vp_ref[...]
    vwin[A_GROUP:2 * A_GROUP, :] = vc_ref[...]
    tail = jnp.zeros((A_BAND_PAD - A_BAND, HEAD_PAIR), BF16)
    kwin[2 * A_GROUP:, :] = tail
    vwin[2 * A_GROUP:, :] = tail

    lane = lax.broadcasted_iota(jnp.int32, (CHUNK, HEAD_PAIR), 1)
    low_half = lane < A_HEAD_DIM
    chunks = A_GROUP // CHUNK

    @pl.when(block == 0)
    def _():
        col = lax.broadcasted_iota(jnp.int32, (2 * CHUNK, A_BAND_PAD), 1)
        for c in range(chunks):
            bias_scr[c] = jnp.where(col >= A_GROUP - c * CHUNK, bias_ref[0], -jnp.inf)

    @pl.when(block == 1)
    def _():
        for c in range(chunks):
            bias_scr[c] = bias_ref[0]

    for group in range(0, chunks, A_INTERLEAVE):
        members = range(group, group + A_INTERLEAVE)
        scores = []
        for c in members:
            qc = q_ref[c * CHUNK:(c + 1) * CHUNK, :]
            q2 = jnp.concatenate([jnp.where(low_half, qc, 0), jnp.where(low_half, 0, qc)], axis=0)
            scores.append(_dot_nt(q2, kwin[c * CHUNK:c * CHUNK + A_BAND_PAD, :]) + bias_scr[c])
        maxes = [jnp.max(s, axis=-1, keepdims=True) for s in scores]
        probs, denoms = [], []
        for s, m in zip(scores, maxes):
            p = jnp.exp2(s - m)
            denoms.append(jnp.sum(p, axis=-1, keepdims=True))
            probs.append(p.astype(BF16))
        outs = [_dot(p, vwin[c * CHUNK:c * CHUNK + A_BAND_PAD, :]) / d
                for c, p, d in zip(members, probs, denoms)]
        for c, out in zip(members, outs):
            o_ref[c * CHUNK:(c + 1) * CHUNK, :] = jnp.where(
                low_half, out[0:CHUNK], out[CHUNK:2 * CHUNK]).astype(BF16)


def _relpos_bias_table(rel_bias):
    rb = rel_bias.astype(F32) * LOG2_E
    far = A_MAX_REL + CHUNK - 1
    ext = jnp.concatenate([jnp.broadcast_to(rb[:, far:far + 1], (A_HEADS, far + 1)),
                           rb[:, far - 1::-1]], axis=1)
    table = jnp.stack([ext[:, CHUNK - 1 - i:CHUNK - 1 - i + A_BAND] for i in range(CHUNK)], axis=1)
    table = jnp.pad(table, ((0, 0), (0, 0), (0, A_BAND_PAD - A_BAND)), constant_values=-jnp.inf)
    return table.reshape(A_HEADS // 2, 2 * CHUNK, A_BAND_PAD)


def _attn_a(q, k, v, rel_bias, batch, seq):
    tokens = q.shape[0]
    blocks = seq // A_GROUP
    pairs = D_MODEL // HEAD_PAIR
    cur = lambda b, h, j: (b * blocks + j, h)
    prev = lambda b, h, j: (b * blocks + jnp.maximum(j - 1, 0), h)
    blk = (A_GROUP, HEAD_PAIR)
    win_rows = 2 * A_GROUP + A_BAND_PAD - A_BAND
    return pl.pallas_call(
        _attn_a_kernel,
        grid=(batch, pairs, blocks),
        in_specs=[pl.BlockSpec(blk, cur), pl.BlockSpec(blk, prev), pl.BlockSpec(blk, cur),
                  pl.BlockSpec(blk, prev), pl.BlockSpec(blk, cur),
                  pl.BlockSpec((1, 2 * CHUNK, A_BAND_PAD), lambda b, h, j: (h, 0, 0))],
        out_specs=pl.BlockSpec(blk, cur),
        out_shape=jax.ShapeDtypeStruct((tokens, D_MODEL), BF16),
        scratch_shapes=[pltpu.VMEM((win_rows, HEAD_PAIR), BF16),
                        pltpu.VMEM((win_rows, HEAD_PAIR), BF16),
                        pltpu.VMEM((A_GROUP // CHUNK, 2 * CHUNK, A_BAND_PAD), F32)],
        compiler_params=_params("parallel", "parallel", "arbitrary"),
        name="relpos_band_attention",
    )(q, k, k, v, v, _relpos_bias_table(rel_bias))


def _mixb_kernel(x_ref, mod_ref, gain_ref, w_ref, vg_ref, ws_ref, bs_ref, o_ref, v_scr):
    h = _normed_input(x_ref, mod_ref, gain_ref)
    tm = h.shape[0]
    slab = 512
    ssq = jnp.zeros((tm, 1), F32)
    for c in range(SG_HALF // slab):
        zc = _gelu(_dot(h, w_ref[:, SG_HALF + c * slab:SG_HALF + (c + 1) * slab]))
        ssq = ssq + jnp.sum(zc * zc, axis=-1, keepdims=True)
        v_scr[:, c * slab:(c + 1) * slab] = zc
    inv_rms = lax.rsqrt(ssq * (1.0 / SG_HALF) + RMS_EPS)

    row = lax.broadcasted_iota(jnp.int32, (SG_WINDOW, SG_WINDOW), 0) // CHUNK
    colc = lax.broadcasted_iota(jnp.int32, (SG_WINDOW, SG_WINDOW), 1) // CHUNK
    block_lower = row >= colc
    gw = SG_GROUP_WIDTH
    for g in range(SG_GROUPS):
        cols = slice(g * gw, (g + 1) * gw)
        w_s = jnp.where(block_lower, ws_ref[g], 0.0).astype(BF16)
        vn = (v_scr[:, cols] * inv_rms * vg_ref[:, cols]).astype(BF16)
        u = _gelu(_dot(h, w_ref[:, g * gw:(g + 1) * gw]))
        for w in range(tm // SG_WINDOW):
            rows = slice(w * SG_WINDOW, (w + 1) * SG_WINDOW)
            vm = _dot(w_s, vn[rows]) + bs_ref[g]
            o_ref[rows, cols] = (u[rows] * vm).astype(BF16)


def _mixb(x2, mod, gain, w_in, v_gain, w_s, b_s, seq):
    tokens = x2.shape[0]
    grid, x_spec, mod_spec = _row_specs(tokens, seq, TM_MIXB)
    bias = jnp.broadcast_to(b_s.astype(F32)[:, :, None], (SG_GROUPS, SG_WINDOW, SG_GROUP_WIDTH))
    out_spec, out_shape = _row_out(tokens, SG_HALF, TM_MIXB)
    return pl.pallas_call(
        _mixb_kernel,
        grid=grid,
        in_specs=[x_spec, mod_spec, _resident((1, D_MODEL)), _resident((D_MODEL, SG_FF)),
                  _resident((1, SG_HALF)), _resident((SG_GROUPS, SG_WINDOW, SG_WINDOW)),
                  _resident((SG_GROUPS, SG_WINDOW, SG_GROUP_WIDTH))],
        out_specs=out_spec, out_shape=out_shape,
        scratch_shapes=[pltpu.VMEM((TM_MIXB, SG_HALF), F32)],
        compiler_params=_params("parallel"),
        name="gmlp_spatial_gating",
    )(x2, mod, gain.reshape(1, D_MODEL), w_in.astype(BF16), v_gain.reshape(1, SG_HALF),
      w_s.astype(F32), bias)


def _gla_decay_matrix():
    r = np.arange(CHUNK)[:, None]
    i = np.arange(CHUNK)[None, :]
    parts = [i <= r, i > r]
    for level in range(GLA_LEVELS):
        n = (CHUNK // 2) >> level
        ref = (r // (2 * n)) * 2 * n + n
        upper = (r % (2 * n)) >= n
        parts.append(np.where(upper, (i > ref) & (i <= r), (i > r) & (i <= ref)))
    m = np.concatenate(parts, axis=0)
    return jnp.asarray(np.concatenate([m, m], axis=1), BF16)


def _gla_level_map():
    t = np.arange(CHUNK)[:, None]
    s = np.arange(CHUNK)[None, :]
    out = np.full((CHUNK, CHUNK), -1, np.int32)
    out[t == s] = 0
    for level in range(GLA_LEVELS):
        n = (CHUNK // 2) >> level
        hit = (t // (2 * n) == s // (2 * n)) & (t % (2 * n) >= n) & (s % (2 * n) < n)
        out[hit] = 1 + level
    return jnp.asarray(out)


def _gla_kernel(q_ref, k_ref, v_ref, r_ref, la_ref, dm_ref, lvl_ref, og_ref, o_ref, state):
    @pl.when(pl.program_id(1) == 0)
    def _():
        state[...] = jnp.zeros_like(state)

    dm = dm_ref[...]
    lvl = lvl_ref[...]

    heads = range(GLA_HEADS)
    kcols = [slice(h * GLA_DK, (h + 1) * GLA_DK) for h in heads]
    vcols = [slice(h * GLA_DV, (h + 1) * GLA_DV) for h in heads]

    def group(g, carry):
        base = g * (GLA_UNROLL * CHUNK)
        chunk_rows = [pl.ds(pl.multiple_of(base + u * CHUNK, CHUNK), CHUNK) for u in range(GLA_UNROLL)]
        decays = []
        for rows in chunk_rows:
            la = la_ref[rows, :]
            la_hi = la.astype(BF16)
            la_lo = (la - la_hi.astype(F32)).astype(BF16)
            decays.append(jnp.exp(_dot(dm, jnp.concatenate([la_hi, la_lo], axis=0))))
        units = [(u, h) for u in range(GLA_UNROLL) for h in heads]
        qs = [q_ref[chunk_rows[u], kcols[h]].astype(F32) for u, h in units]
        ks = [k_ref[chunk_rows[u], kcols[h]].astype(F32) for u, h in units]
        pairs = []
        for (u, h), qh, kh in zip(units, qs, ks):
            level_pairs = [_dot_nt(qh.astype(BF16), kh.astype(BF16))]
            for level in range(GLA_LEVELS):
                e = decays[u][(2 + level) * CHUNK:(3 + level) * CHUNK, kcols[h]]
                level_pairs.append(_dot_nt((qh * e).astype(BF16), (kh * e).astype(BF16)))
            pairs.append(level_pairs)
        intra = []
        for (u, h), level_pairs in zip(units, pairs):
            att = jnp.where(lvl == 0, level_pairs[0], 0.0)
            for level in range(GLA_LEVELS):
                att = jnp.where(lvl == 1 + level, level_pairs[1 + level], att)
            intra.append(_dot(att.astype(BF16), v_ref[chunk_rows[u], vcols[h]]))
        states = [state[h] for h in heads]
        for idx, (u, h) in enumerate(units):
            rows, kc, vc = chunk_rows[u], kcols[h], vcols[h]
            decay = decays[u]
            o = intra[idx] + _dot_nt((qs[idx] * decay[0:CHUNK, kc]).astype(BF16), states[h].astype(BF16))
            k_dec = (ks[idx] * decay[CHUNK:2 * CHUNK, kc]).astype(BF16)
            states[h] = (states[h] * decay[CHUNK - 1:CHUNK, kc]
                         + lax.dot_general(v_ref[rows, vc], k_dec, _TN, preferred_element_type=F32))
            ms = jnp.mean(o * o, axis=-1, keepdims=True)
            on = o * lax.rsqrt(ms + RMS_EPS) * og_ref[:, vc]
            o_ref[rows, vc] = (jax.nn.silu(r_ref[rows, vc].astype(F32)) * on).astype(BF16)
        for h in heads:
            state[h] = states[h]
        return carry

    lax.fori_loop(0, GLA_STEP_CHUNKS // GLA_UNROLL, group, 0)


def _gla(q, k, v, r, la, o_gain, batch, seq):
    tokens = q.shape[0]
    step = GLA_STEP_CHUNKS * CHUNK
    blocks = seq // step
    idx = lambda b, j: (b * blocks + j, 0)
    return pl.pallas_call(
        _gla_kernel,
        grid=(batch, blocks),
        in_specs=[pl.BlockSpec((step, GLA_KEY_WIDTH), idx), pl.BlockSpec((step, GLA_KEY_WIDTH), idx),
                  pl.BlockSpec((step, D_MODEL), idx), pl.BlockSpec((step, D_MODEL), idx),
                  pl.BlockSpec((step, GLA_KEY_WIDTH), idx),
                  _resident((8 * CHUNK, 2 * CHUNK)), _resident((CHUNK, CHUNK)), _resident((1, D_MODEL))],
        out_specs=pl.BlockSpec((step, D_MODEL), idx),
        out_shape=jax.ShapeDtypeStruct((tokens, D_MODEL), BF16),
        scratch_shapes=[pltpu.VMEM((GLA_HEADS, GLA_DV, GLA_DK), F32)],
        compiler_params=_params("parallel", "arbitrary"),
        name="gla_scan",
    )(q, k, v, r, la, _gla_decay_matrix(), _gla_level_map(),
      jnp.tile(o_gain, GLA_HEADS).reshape(1, D_MODEL))


def _sb_kernel(q_ref, k_ref, v_ref, tri_ref, o_ref, z_scr, w_scr, acc_scr, later_scr):
    qb = pl.program_id(2)
    diag_visits = SB_QROWS // SB_KBLOCK
    visits = diag_visits * (qb + 1)
    last_block = k_ref.shape[0] // SB_KBLOCK - 1
    tri = tri_ref[...]
    lane = lax.broadcasted_iota(jnp.int32, (SB_QROWS, HEAD_PAIR), 1)
    low_half = lane < SB_HEAD_DIM
    q = q_ref[...]
    q_heads = (jnp.where(low_half, q, 0), jnp.where(low_half, 0, q))
    sign_bit = jnp.uint32(0x80000000)
    inv_ln2 = 1.0 / np.log(2.0)
    w_scr[...] = jnp.zeros_like(w_scr)
    acc_scr[...] = jnp.zeros_like(acc_scr)
    later_scr[...] = jnp.zeros_like(later_scr)

    def key_rows(kb):
        return pl.ds(pl.multiple_of(kb * SB_KBLOCK, SB_KBLOCK), SB_KBLOCK)

    def scores(kb, slot, first_row=0):
        kblk = k_ref[key_rows(kb), :]
        for h in range(2):
            z_scr[slot, h, first_row:, :] = _dot_nt(q_heads[h][first_row:], kblk)

    def visit(v, slot, masked, first_row=0):
        kb = visits - 1 - v
        scores(jnp.maximum(kb - 1, 0), 1 - slot)
        vprev = v_ref[key_rows(jnp.minimum(kb + 1, last_block)), :]
        for h in range(2):
            acc_scr[h] += _dot(w_scr[h], vprev)
        if masked:
            t_idx = lax.broadcasted_iota(jnp.int32, (SB_STRIP, SB_KBLOCK), 0) + qb * SB_QROWS
            s_idx = lax.broadcasted_iota(jnp.int32, (SB_STRIP, SB_KBLOCK), 1) + kb * SB_KBLOCK
        strips = [slice(r, r + SB_STRIP) for r in range(first_row, SB_QROWS, SB_STRIP)]
        incls = []
        for h in range(2):
            sps = []
            for rows in strips:
                zr = z_scr[slot, h, rows, :]
                neg_abs = lax.bitcast_convert_type(lax.bitcast_convert_type(zr, jnp.uint32) | sign_bit, F32)
                sp = jnp.maximum(zr, 0.0) + jnp.log(1.0 + jnp.exp2(neg_abs)) * inv_ln2
                if masked:
                    sp = jnp.where(s_idx < t_idx + rows.start, sp, 0.0)
                sps.append(sp.astype(BF16))
            incls.append(_dot(jnp.concatenate(sps, axis=0), tri))
        for h in range(2):
            for rows in strips:
                incl = incls[h][rows.start - first_row:rows.stop - first_row]
                w = jnp.exp2(z_scr[slot, h, rows, :] - incl - later_scr[h, rows, :])
                if masked:
                    w = jnp.where(s_idx < t_idx + rows.start, w, 0.0)
                w_scr[h, rows, :] = w.astype(BF16)
                later_scr[h, rows, :] += incl[:, 0:1]

    def finished(v):
        return jnp.logical_or(v >= visits, jnp.min(later_scr[...]) >= SB_DEAD_LOG2).astype(jnp.int32)

    scores(visits - 1, 0, first_row=SB_QROWS - SB_KBLOCK)
    visit(0, 0, True, first_row=SB_QROWS - SB_KBLOCK)
    visit(1, 1, True)

    def step(carry):
        v, _ = carry
        visit(v, 0, False)
        visit(v + 1, 1, False)
        return v + 2, finished(v + 2)

    assert diag_visits == 2
    v_end, _ = lax.while_loop(lambda carry: carry[1] == 0, step,
                              (jnp.int32(diag_visits), (visits <= diag_visits).astype(jnp.int32)))
    pending = v_ref[key_rows(visits - v_end), :]
    out = [acc_scr[h] + _dot(w_scr[h], pending) for h in range(2)]
    o_ref[...] = jnp.where(low_half, out[0], out[1]).astype(BF16)


def _stick_breaking(q, k, v, batch, seq):
    tokens = q.shape[0]
    blocks = seq // SB_QROWS
    pairs = D_MODEL // HEAD_PAIR
    j = np.arange(SB_KBLOCK)
    tri = jnp.asarray(j[:, None] >= j[None, :], BF16)
    qo_spec = pl.BlockSpec((SB_QROWS, HEAD_PAIR), lambda b, h, i: (b * blocks + i, h))
    kv_spec = pl.BlockSpec((seq, HEAD_PAIR), lambda b, h, i: (b, h))
    return pl.pallas_call(
        _sb_kernel,
        grid=(batch, pairs, blocks),
        in_specs=[qo_spec, kv_spec, kv_spec, _resident((SB_KBLOCK, SB_KBLOCK))],
        out_specs=qo_spec,
        out_shape=jax.ShapeDtypeStruct((tokens, D_MODEL), BF16),
        scratch_shapes=[pltpu.VMEM((2, 2, SB_QROWS, SB_KBLOCK), F32),
                        pltpu.VMEM((2, SB_QROWS, SB_KBLOCK), BF16),
                        pltpu.VMEM((2, SB_QROWS, HEAD_PAIR), F32),
                        pltpu.VMEM((2, SB_QROWS, 1), F32)],
        compiler_params=_params("parallel", "parallel", "arbitrary"),
        name="stick_breaking_attention",
    )(q, k, v, tri)


def _post_kernel(o_ref, x_ref, mod_ref, gain_ref, wo_ref, w1_ref, w2_ref, out_ref):
    gate1 = mod_ref[0, 2:3, :]
    shift2, scale2, gate2 = mod_ref[0, 3:4, :], mod_ref[0, 4:5, :], mod_ref[0, 5:6, :]
    x1 = x_ref[...] + gate1 * _dot(o_ref[...], wo_ref[...])
    h = _modulated_norm(x1, gain_ref[...], scale2, shift2).astype(BF16)
    acc = jnp.zeros_like(x1)
    for c in range(FFN_HIDDEN // FFN_SLAB):
        t = _dot(h, w1_ref[:, c * FFN_SLAB:(c + 1) * FFN_SLAB])
        t = jnp.square(jnp.maximum(t, 0.0)).astype(BF16)
        acc = acc + _dot(t, w2_ref[c * FFN_SLAB:(c + 1) * FFN_SLAB, :])
    out_ref[...] = x1 + gate2 * acc


def _post(o, x2, mod, gain, w_out, w1, w2, seq):
    tokens = x2.shape[0]
    grid, x_spec, mod_spec = _row_specs(tokens, seq, TM_PROJ)
    width = o.shape[1]
    out_spec, out_shape = _row_out(tokens, D_MODEL, TM_PROJ, F32)
    return pl.pallas_call(
        _post_kernel,
        grid=grid,
        in_specs=[pl.BlockSpec((TM_PROJ, width), lambda i: (i, 0)), x_spec, mod_spec,
                  _resident((1, D_MODEL)), _resident((width, D_MODEL)),
                  _resident((D_MODEL, FFN_HIDDEN)), _resident((FFN_HIDDEN, D_MODEL))],
        out_specs=out_spec, out_shape=out_shape,
        compiler_params=_params("parallel"),
        name="outproj_mlp",
    )(o, x2, mod, gain.reshape(1, D_MODEL), w_out.astype(BF16), w1.astype(BF16), w2.astype(BF16))


def kernel(x, c, ada_w, ada_b, norm_mix, norm_ffn, ffn_w1, ffn_w2, a_w_in, a_q_gain, a_k_gain, a_rel_bias, a_w_out, b_w_in, b_v_gain, b_w_s, b_b_s, b_w_out, c_w_in, c_w_gate_up, c_b_gate, c_o_gain, c_w_out, d_w_in, d_w_out):
    batch, seq, _ = x.shape
    x2 = x.reshape(batch * seq, D_MODEL)
    mods = _modulation(c, ada_w, ada_b)
    for i in range(DEPTH):
        m, j = i % N_MIXERS, i // N_MIXERS
        mod = mods[i]
        if m == 0:
            q, k, v = _inproj_a(x2, mod, norm_mix[i], a_w_in[j], a_q_gain[j], a_k_gain[j], seq)
            o = _attn_a(q, k, v, a_rel_bias[j], batch, seq)
            w_out = a_w_out[j]
        elif m == 1:
            o = _mixb(x2, mod, norm_mix[i], b_w_in[j], b_v_gain[j], b_w_s[j], b_b_s[j], seq)
            w_out = b_w_out[j]
        elif m == 2:
            q, k, v, r, la = _inproj_c(x2, mod, norm_mix[i], c_w_in[j], c_w_gate_up[j], c_b_gate[j], seq)
            o = _gla(q, k, v, r, la, c_o_gain[j], batch, seq)
            w_out = c_w_out[j]
        else:
            q, k, v = _inproj_d(x2, mod, norm_mix[i], d_w_in[j], seq)
            o = _stick_breaking(q, k, v, batch, seq)
            w_out = d_w_out[j]
        x2 = _post(o, x2, mod, norm_ffn[i], w_out, ffn_w1[i], ffn_w2[i], seq)
    return x2.reshape(batch, seq, D_MODEL)
```

```python
import jax
import jax.numpy as jnp
import numpy as np
from jax import lax
from jax.experimental import pallas as pl
from jax.experimental.pallas import tpu as pltpu

F32 = jnp.float32
BF16 = jnp.bfloat16

D_MODEL = 1024
DEPTH = 4
N_MIXERS = 4
CHUNK = 64
RMS_EPS = 1e-6

A_HEADS = 16
A_HEAD_DIM = 64
A_LEFT_CHUNKS = 8
A_MAX_REL = 256
A_BAND = (A_LEFT_CHUNKS + 1) * CHUNK
A_BAND_PAD = 640
A_GROUP = 8 * CHUNK
A_INTERLEAVE = 8

SG_FF = 6 * D_MODEL
SG_HALF = SG_FF // 2
SG_GROUPS = 8
SG_WINDOW = 128
SG_GROUP_WIDTH = SG_HALF // SG_GROUPS

GLA_HEADS = 4
GLA_KEY_WIDTH = D_MODEL // 2
GLA_DK = 128
GLA_DV = 256
GLA_GATE_RANK = 16
GLA_GATE_PAD = 128
GLA_TAU = 16.0
GLA_LEVELS = 6
GLA_STEP_CHUNKS = 8
GLA_UNROLL = 4

SB_HEADS = 16
SB_HEAD_DIM = 64
SB_QROWS = 512
SB_KBLOCK = 256
SB_STRIP = 64
SB_DEAD_LOG2 = 160.0
LOG2_E = float(np.log2(np.e))
SB_Q_SCALE = SB_HEAD_DIM ** -0.5 * LOG2_E
A_Q_SCALE = A_HEAD_DIM ** -0.5 * LOG2_E

FFN_HIDDEN = 4 * D_MODEL
FFN_SLAB = 1024

LANES = 128
SUBLANES = 8
MXU_TILE = 256
PROJ_SLAB = 512
MOD_COLS = 1536
HEAD_PAIR = LANES
VMEM_LIMIT_BYTES = 56 * 1024 * 1024

TM_PROJ = 512
TM_MIXB = 256

_NT = (((1,), (1,)), ((), ()))
_TN = (((0,), (0,)), ((), ()))


def _params(*semantics):
    return pltpu.CompilerParams(dimension_semantics=semantics,
                                vmem_limit_bytes=VMEM_LIMIT_BYTES)


def _dot(a, b):
    return jnp.dot(a, b, preferred_element_type=F32)


def _dot_nt(a, b):
    return lax.dot_general(a, b, _NT, preferred_element_type=F32)


def _resident(shape):
    zeros = (0,) * len(shape)
    return pl.BlockSpec(shape, lambda *_: zeros, pipeline_mode=pl.Buffered(1))


def _gelu(x):
    return 0.5 * x * (1.0 + lax.erf(x * (2.0 ** -0.5)))


def _modulated_norm(x, gain, scale, shift):
    ms = jnp.mean(x * x, axis=-1, keepdims=True)
    y = x * lax.rsqrt(ms + RMS_EPS) * gain
    return y * (1.0 + scale) + shift


def _mod_kernel(c_ref, w_ref, b_ref, o_ref):
    cond = jax.nn.silu(c_ref[...])
    o_ref[0] = jnp.dot(cond, w_ref[0], precision=lax.Precision.HIGHEST,
                       preferred_element_type=F32) + b_ref[0]


def _modulation(c, ada_w, ada_b):
    batch = c.shape[0]
    rows = SUBLANES
    c_pad = jnp.zeros((rows, D_MODEL), F32).at[:batch].set(c)
    tn = MOD_COLS
    out = pl.pallas_call(
        _mod_kernel,
        grid=(DEPTH, 6 * D_MODEL // tn),
        in_specs=[
            pl.BlockSpec((rows, D_MODEL), lambda i, j: (0, 0)),
            pl.BlockSpec((1, D_MODEL, tn), lambda i, j: (i, 0, j)),
            pl.BlockSpec((1, 1, tn), lambda i, j: (i, 0, j)),
        ],
        out_specs=pl.BlockSpec((1, rows, tn), lambda i, j: (i, 0, j)),
        out_shape=jax.ShapeDtypeStruct((DEPTH, rows, 6 * D_MODEL), F32),
        compiler_params=_params("parallel", "parallel"),
        name="adaln_modulation",
    )(c_pad, ada_w, ada_b.reshape(DEPTH, 1, 6 * D_MODEL))
    return out[:, :batch].reshape(DEPTH, batch, 6, D_MODEL)


def _normed_input(x_ref, mod_ref, gain_ref):
    shift, scale = mod_ref[0, 0:1, :], mod_ref[0, 1:2, :]
    return _modulated_norm(x_ref[...], gain_ref[...], scale, shift).astype(BF16)


def _inproj_a_kernel(x_ref, mod_ref, gain_ref, w_ref, qg_ref, kg_ref, seg_ref,
                     q_ref, k_ref, v_ref):
    h = _normed_input(x_ref, mod_ref, gain_ref)
    seg = seg_ref[...]

    def head_norm(y, gain):
        ssq = _dot((y * y).astype(BF16), seg)
        return y * lax.rsqrt(ssq * (1.0 / A_HEAD_DIM) + RMS_EPS) * gain

    width = seg.shape[0]
    slabs = [(part, c) for c in range(D_MODEL // width) for part in range(3)]

    def project(part, c):
        start = part * D_MODEL + c * width
        return _dot(h, w_ref[:, start:start + width])

    y_next = project(*slabs[0])
    for i, (part, c) in enumerate(slabs):
        y = y_next
        if i + 1 < len(slabs):
            y_next = project(*slabs[i + 1])
        cols = slice(c * width, (c + 1) * width)
        if part == 0:
            q_ref[:, cols] = (head_norm(y, qg_ref[:, cols]) * A_Q_SCALE).astype(BF16)
        elif part == 1:
            k_ref[:, cols] = head_norm(y, kg_ref[:, cols]).astype(BF16)
        else:
            v_ref[:, cols] = y.astype(BF16)


def _inproj_d_kernel(x_ref, mod_ref, gain_ref, w_ref, q_ref, k_ref, v_ref):
    h = _normed_input(x_ref, mod_ref, gain_ref)
    width = PROJ_SLAB
    for c in range(D_MODEL // width):
        cols = slice(c * width, (c + 1) * width)
        yq = _dot(h, w_ref[:, c * width:(c + 1) * width])
        q_ref[:, cols] = (yq * SB_Q_SCALE).astype(BF16)
        yk = _dot(h, w_ref[:, D_MODEL + c * width:D_MODEL + (c + 1) * width])
        k_ref[:, cols] = yk.astype(BF16)
        yv = _dot(h, w_ref[:, 2 * D_MODEL + c * width:2 * D_MODEL + (c + 1) * width])
        v_ref[:, cols] = yv.astype(BF16)


def _inproj_c_kernel(x_ref, mod_ref, gain_ref, w_ref, wg_ref, bg_ref,
                     q_ref, k_ref, v_ref, r_ref, la_ref):
    h = _normed_input(x_ref, mod_ref, gain_ref)
    kw = GLA_KEY_WIDTH
    a = _dot(h, w_ref[:, 6 * kw:6 * kw + GLA_GATE_PAD]).astype(BF16)
    y_next = _dot(h, w_ref[:, 0:kw])
    gate = _dot(a, wg_ref[...]) + bg_ref[...]
    la_ref[...] = jax.nn.log_sigmoid(gate) * (1.0 / GLA_TAU)
    outs = [(q_ref, slice(0, kw)), (k_ref, slice(0, kw)), (v_ref, slice(0, kw)), (v_ref, slice(kw, 2 * kw)),
            (r_ref, slice(0, kw)), (r_ref, slice(kw, 2 * kw))]
    for i, (ref, cols) in enumerate(outs):
        y = y_next
        if i + 1 < len(outs):
            y_next = _dot(h, w_ref[:, (i + 1) * kw:(i + 2) * kw])
        if i == 0:
            y = y * (GLA_DK ** -0.5)
        ref[:, cols] = y.astype(BF16)


def _row_specs(tokens, seq, tm):
    tiles_per_batch = seq // tm
    x_spec = pl.BlockSpec((tm, D_MODEL), lambda i: (i, 0))
    mod_spec = pl.BlockSpec((1, 6, D_MODEL), lambda i: (i // tiles_per_batch, 0, 0))
    return (tokens // tm,), x_spec, mod_spec


def _row_out(tokens, width, tm, dtype=BF16):
    return (pl.BlockSpec((tm, width), lambda i: (i, 0)),
            jax.ShapeDtypeStruct((tokens, width), dtype))


def _inproj_a(x2, mod, gain, w_in, q_gain, k_gain, seq):
    tokens = x2.shape[0]
    grid, x_spec, mod_spec = _row_specs(tokens, seq, TM_PROJ)
    seg_width = MXU_TILE
    seg_id = np.arange(seg_width) // A_HEAD_DIM
    seg = jnp.asarray(seg_id[:, None] == seg_id[None, :], BF16)
    specs, shapes = zip(*[_row_out(tokens, D_MODEL, TM_PROJ)] * 3)
    return pl.pallas_call(
        _inproj_a_kernel,
        grid=grid,
        in_specs=[x_spec, mod_spec, _resident((1, D_MODEL)), _resident((D_MODEL, 3 * D_MODEL)),
                  _resident((1, D_MODEL)), _resident((1, D_MODEL)),
                  _resident((seg_width, seg_width))],
        out_specs=list(specs), out_shape=list(shapes),
        compiler_params=_params("parallel"),
        name="inproj_relpos_attention",
    )(x2, mod, gain.reshape(1, D_MODEL), w_in.astype(BF16),
      jnp.tile(q_gain, A_HEADS).reshape(1, D_MODEL), jnp.tile(k_gain, A_HEADS).reshape(1, D_MODEL), seg)


def _inproj_d(x2, mod, gain, w_in, seq):
    tokens = x2.shape[0]
    grid, x_spec, mod_spec = _row_specs(tokens, seq, TM_PROJ)
    specs, shapes = zip(*[_row_out(tokens, D_MODEL, TM_PROJ)] * 3)
    return pl.pallas_call(
        _inproj_d_kernel,
        grid=grid,
        in_specs=[x_spec, mod_spec, _resident((1, D_MODEL)), _resident((D_MODEL, 3 * D_MODEL))],
        out_specs=list(specs), out_shape=list(shapes),
        compiler_params=_params("parallel"),
        name="inproj_stick_breaking",
    )(x2, mod, gain.reshape(1, D_MODEL), w_in.astype(BF16))


def _inproj_c(x2, mod, gain, w_in, w_gate_up, b_gate, seq):
    tokens = x2.shape[0]
    grid, x_spec, mod_spec = _row_specs(tokens, seq, TM_PROJ)
    main = 2 * GLA_KEY_WIDTH + 2 * D_MODEL
    w_pad = jnp.zeros((D_MODEL, main + GLA_GATE_PAD), BF16).at[:, :main + GLA_GATE_RANK].set(
        w_in.astype(BF16))
    wg_pad = jnp.zeros((GLA_GATE_PAD, GLA_KEY_WIDTH), BF16).at[:GLA_GATE_RANK].set(
        w_gate_up.astype(BF16))
    outs = [_row_out(tokens, GLA_KEY_WIDTH, TM_PROJ), _row_out(tokens, GLA_KEY_WIDTH, TM_PROJ),
            _row_out(tokens, D_MODEL, TM_PROJ), _row_out(tokens, D_MODEL, TM_PROJ),
            _row_out(tokens, GLA_KEY_WIDTH, TM_PROJ, F32)]
    specs, shapes = zip(*outs)
    return pl.pallas_call(
        _inproj_c_kernel,
        grid=grid,
        in_specs=[x_spec, mod_spec, _resident((1, D_MODEL)),
                  _resident((D_MODEL, main + GLA_GATE_PAD)),
                  _resident((GLA_GATE_PAD, GLA_KEY_WIDTH)), _resident((1, GLA_KEY_WIDTH))],
        out_specs=list(specs), out_shape=list(shapes),
        compiler_params=_params("parallel"),
        name="inproj_gla",
    )(x2, mod, gain.reshape(1, D_MODEL), w_pad, wg_pad, b_gate.reshape(1, GLA_KEY_WIDTH))


def _attn_a_kernel(q_ref, kp_ref, kc_ref, vp_ref, vc_ref, bias_ref, o_ref, kwin, vwin, bias_scr):
    block = pl.program_id(2)
    kwin[0:A_GROUP, :] = kp_ref[...]
    kwin[A_GROUP:2 * A_GROUP, :] = kc_ref[...]
    vwin[0:A_GROUP, :] = vp_ref[...]
    vwin[A_GROUP:2 * A_GROUP, :] = vc_ref[...]
    tail = jnp.zeros((A_BAND_PAD - A_BAND, HEAD_PAIR), BF16)
    kwin[2 * A_GROUP:, :] = tail
    vwin[2 * A_GROUP:, :] = tail

    lane = lax.broadcasted_iota(jnp.int32, (CHUNK, HEAD_PAIR), 1)
    low_half = lane < A_HEAD_DIM
    chunks = A_GROUP // CHUNK

    @pl.when(block == 0)
    def _():
        col = lax.broadcasted_iota(jnp.int32, (2 * CHUNK, A_BAND_PAD), 1)
        for c in range(chunks):
            bias_scr[c] = jnp.where(col >= A_GROUP - c * CHUNK, bias_ref[0], -jnp.inf)

    @pl.when(block == 1)
    def _():
        for c in range(chunks):
            bias_scr[c] = bias_ref[0]

    for group in range(0, chunks, A_INTERLEAVE):
        members = range(group, group + A_INTERLEAVE)
        scores = []
        for c in members:
            qc = q_ref[c * CHUNK:(c + 1) * CHUNK, :]
            q2 = jnp.concatenate([jnp.where(low_half, qc, 0), jnp.where(low_half, 0, qc)], axis=0)
            scores.append(_dot_nt(q2, kwin[c * CHUNK:c * CHUNK + A_BAND_PAD, :]) + bias_scr[c])
        maxes = [jnp.max(s, axis=-1, keepdims=True) for s in scores]
        probs, denoms = [], []
        for s, m in zip(scores, maxes):
            p = jnp.exp2(s - m)
            denoms.append(jnp.sum(p, axis=-1, keepdims=True))
            probs.append(p.astype(BF16))
        outs = [_dot(p, vwin[c * CHUNK:c * CHUNK + A_BAND_PAD, :]) / d
                for c, p, d in zip(members, probs, denoms)]
        for c, out in zip(members, outs):
            o_ref[c * CHUNK:(c + 1) * CHUNK, :] = jnp.where(
                low_half, out[0:CHUNK], out[CHUNK:2 * CHUNK]).astype(BF16)


def _relpos_bias_table(rel_bias):
    rb = rel_bias.astype(F32) * LOG2_E
    far = A_MAX_REL + CHUNK - 1
    ext = jnp.concatenate([jnp.broadcast_to(rb[:, far:far + 1], (A_HEADS, far + 1)),
                           rb[:, far - 1::-1]], axis=1)
    table = jnp.stack([ext[:, CHUNK - 1 - i:CHUNK - 1 - i + A_BAND] for i in range(CHUNK)], axis=1)
    table = jnp.pad(table, ((0, 0), (0, 0), (0, A_BAND_PAD - A_BAND)), constant_values=-jnp.inf)
    return table.reshape(A_HEADS // 2, 2 * CHUNK, A_BAND_PAD)


def _attn_a(q, k, v, rel_bias, batch, seq):
    tokens = q.shape[0]
    blocks = seq // A_GROUP
    pairs = D_MODEL // HEAD_PAIR
    cur = lambda b, h, j: (b * blocks + j, h)
    prev = lambda b, h, j: (b * blocks + jnp.maximum(j - 1, 0), h)
    blk = (A_GROUP, HEAD_PAIR)
    win_rows = 2 * A_GROUP + A_BAND_PAD - A_BAND
    return pl.pallas_call(
        _attn_a_kernel,
        grid=(batch, pairs, blocks),
        in_specs=[pl.BlockSpec(blk, cur), pl.BlockSpec(blk, prev), pl.BlockSpec(blk, cur),
                  pl.BlockSpec(blk, prev), pl.BlockSpec(blk, cur),
                  pl.BlockSpec((1, 2 * CHUNK, A_BAND_PAD), lambda b, h, j: (h, 0, 0))],
        out_specs=pl.BlockSpec(blk, cur),
        out_shape=jax.ShapeDtypeStruct((tokens, D_MODEL), BF16),
        scratch_shapes=[pltpu.VMEM((win_rows, HEAD_PAIR), BF16),
                        pltpu.VMEM((win_rows, HEAD_PAIR), BF16),
                        pltpu.VMEM((A_GROUP // CHUNK, 2 * CHUNK, A_BAND_PAD), F32)],
        compiler_params=_params("parallel", "parallel", "arbitrary"),
        name="relpos_band_attention",
    )(q, k, k, v, v, _relpos_bias_table(rel_bias))


def _mixb_kernel(x_ref, mod_ref, gain_ref, w_ref, vg_ref, ws_ref, bs_ref, o_ref, v_scr):
    h = _normed_input(x_ref, mod_ref, gain_ref)
    tm = h.shape[0]
    slab = PROJ_SLAB
    ssq = jnp.zeros((tm, 1), F32)
    for c in range(SG_HALF // slab):
        zc = _gelu(_dot(h, w_ref[:, SG_HALF + c * slab:SG_HALF + (c + 1) * slab]))
        ssq = ssq + jnp.sum(zc * zc, axis=-1, keepdims=True)
        v_scr[:, c * slab:(c + 1) * slab] = zc
    inv_rms = lax.rsqrt(ssq * (1.0 / SG_HALF) + RMS_EPS)

    row = lax.broadcasted_iota(jnp.int32, (SG_WINDOW, SG_WINDOW), 0) // CHUNK
    colc = lax.broadcasted_iota(jnp.int32, (SG_WINDOW, SG_WINDOW), 1) // CHUNK
    block_lower = row >= colc
    gw = SG_GROUP_WIDTH
    for g in range(SG_GROUPS):
        cols = slice(g * gw, (g + 1) * gw)
        w_s = jnp.where(block_lower, ws_ref[g], 0.0).astype(BF16)
        vn = (v_scr[:, cols] * inv_rms * vg_ref[:, cols]).astype(BF16)
        u = _gelu(_dot(h, w_ref[:, g * gw:(g + 1) * gw]))
        for w in range(tm // SG_WINDOW):
            rows = slice(w * SG_WINDOW, (w + 1) * SG_WINDOW)
            vm = _dot(w_s, vn[rows]) + bs_ref[g]
            o_ref[rows, cols] = (u[rows] * vm).astype(BF16)


def _mixb(x2, mod, gain, w_in, v_gain, w_s, b_s, seq):
    tokens = x2.shape[0]
    grid, x_spec, mod_spec = _row_specs(tokens, seq, TM_MIXB)
    bias = jnp.broadcast_to(b_s.astype(F32)[:, :, None], (SG_GROUPS, SG_WINDOW, SG_GROUP_WIDTH))
    out_spec, out_shape = _row_out(tokens, SG_HALF, TM_MIXB)
    return pl.pallas_call(
        _mixb_kernel,
        grid=grid,
        in_specs=[x_spec, mod_spec, _resident((1, D_MODEL)), _resident((D_MODEL, SG_FF)),
                  _resident((1, SG_HALF)), _resident((SG_GROUPS, SG_WINDOW, SG_WINDOW)),
                  _resident((SG_GROUPS, SG_WINDOW, SG_GROUP_WIDTH))],
        out_specs=out_spec, out_shape=out_shape,
        scratch_shapes=[pltpu.VMEM((TM_MIXB, SG_HALF), F32)],
        compiler_params=_params("parallel"),
        name="gmlp_spatial_gating",
    )(x2, mod, gain.reshape(1, D_MODEL), w_in.astype(BF16), v_gain.reshape(1, SG_HALF),
      w_s.astype(F32), bias)


def _gla_decay_matrix():
    r = np.arange(CHUNK)[:, None]
    i = np.arange(CHUNK)[None, :]
    parts = [i <= r, i > r]
    for level in range(GLA_LEVELS):
        n = (CHUNK // 2) >> level
        ref = (r // (2 * n)) * 2 * n + n
        upper = (r % (2 * n)) >= n
        parts.append(np.where(upper, (i > ref) & (i <= r), (i > r) & (i <= ref)))
    m = np.concatenate(parts, axis=0)
    return jnp.asarray(np.concatenate([m, m], axis=1), BF16)


def _gla_level_map():
    t = np.arange(CHUNK)[:, None]
    s = np.arange(CHUNK)[None, :]
    out = np.full((CHUNK, CHUNK), -1, np.int32)
    out[t == s] = 0
    for level in range(GLA_LEVELS):
        n = (CHUNK // 2) >> level
        hit = (t // (2 * n) == s // (2 * n)) & (t % (2 * n) >= n) & (s % (2 * n) < n)
        out[hit] = 1 + level
    return jnp.asarray(out)


def _gla_kernel(q_ref, k_ref, v_ref, r_ref, la_ref, dm_ref, lvl_ref, og_ref, o_ref, state):
    @pl.when(pl.program_id(1) == 0)
    def _():
        state[...] = jnp.zeros_like(state)

    dm = dm_ref[...]
    lvl = lvl_ref[...]

    heads = range(GLA_HEADS)
    kcols = [slice(h * GLA_DK, (h + 1) * GLA_DK) for h in heads]
    vcols = [slice(h * GLA_DV, (h + 1) * GLA_DV) for h in heads]

    def group(g, carry):
        base = g * (GLA_UNROLL * CHUNK)
        chunk_rows = [pl.ds(pl.multiple_of(base + u * CHUNK, CHUNK), CHUNK) for u in range(GLA_UNROLL)]
        decays = []
        for rows in chunk_rows:
            la = la_ref[rows, :]
            la_hi = la.astype(BF16)
            la_lo = (la - la_hi.astype(F32)).astype(BF16)
            decays.append(jnp.exp(_dot(dm, jnp.concatenate([la_hi, la_lo], axis=0))))
        units = [(u, h) for u in range(GLA_UNROLL) for h in heads]
        qs = [q_ref[chunk_rows[u], kcols[h]].astype(F32) for u, h in units]
        ks = [k_ref[chunk_rows[u], kcols[h]].astype(F32) for u, h in units]
        pairs = []
        for (u, h), qh, kh in zip(units, qs, ks):
            level_pairs = [_dot_nt(qh.astype(BF16), kh.astype(BF16))]
            for level in range(GLA_LEVELS):
                e = decays[u][(2 + level) * CHUNK:(3 + level) * CHUNK, kcols[h]]
                level_pairs.append(_dot_nt((qh * e).astype(BF16), (kh * e).astype(BF16)))
            pairs.append(level_pairs)
        intra = []
        for (u, h), level_pairs in zip(units, pairs):
            att = jnp.where(lvl == 0, level_pairs[0], 0.0)
            for level in range(GLA_LEVELS):
                att = jnp.where(lvl == 1 + level, level_pairs[1 + level], att)
            intra.append(_dot(att.astype(BF16), v_ref[chunk_rows[u], vcols[h]]))
        states = [state[h] for h in heads]
        for idx, (u, h) in enumerate(units):
            rows, kc, vc = chunk_rows[u], kcols[h], vcols[h]
            decay = decays[u]
            o = intra[idx] + _dot_nt((qs[idx] * decay[0:CHUNK, kc]).astype(BF16), states[h].astype(BF16))
            k_dec = (ks[idx] * decay[CHUNK:2 * CHUNK, kc]).astype(BF16)
            states[h] = (states[h] * decay[CHUNK - 1:CHUNK, kc]
                         + lax.dot_general(v_ref[rows, vc], k_dec, _TN, preferred_element_type=F32))
            ms = jnp.mean(o * o, axis=-1, keepdims=True)
            on = o * lax.rsqrt(ms + RMS_EPS) * og_ref[:, vc]
            o_ref[rows, vc] = (jax.nn.silu(r_ref[rows, vc].astype(F32)) * on).astype(BF16)
        for h in heads:
            state[h] = states[h]
        return carry

    lax.fori_loop(0, GLA_STEP_CHUNKS // GLA_UNROLL, group, 0)


def _gla(q, k, v, r, la, o_gain, batch, seq):
    tokens = q.shape[0]
    step = GLA_STEP_CHUNKS * CHUNK
    blocks = seq // step
    idx = lambda b, j: (b * blocks + j, 0)
    return pl.pallas_call(
        _gla_kernel,
        grid=(batch, blocks),
        in_specs=[pl.BlockSpec((step, GLA_KEY_WIDTH), idx), pl.BlockSpec((step, GLA_KEY_WIDTH), idx),
                  pl.BlockSpec((step, D_MODEL), idx), pl.BlockSpec((step, D_MODEL), idx),
                  pl.BlockSpec((step, GLA_KEY_WIDTH), idx),
                  _resident((8 * CHUNK, 2 * CHUNK)), _resident((CHUNK, CHUNK)), _resident((1, D_MODEL))],
        out_specs=pl.BlockSpec((step, D_MODEL), idx),
        out_shape=jax.ShapeDtypeStruct((tokens, D_MODEL), BF16),
        scratch_shapes=[pltpu.VMEM((GLA_HEADS, GLA_DV, GLA_DK), F32)],
        compiler_params=_params("parallel", "arbitrary"),
        name="gla_scan",
    )(q, k, v, r, la, _gla_decay_matrix(), _gla_level_map(),
      jnp.tile(o_gain, GLA_HEADS).reshape(1, D_MODEL))


def _sb_kernel(q_ref, k_ref, v_ref, tri_ref, o_ref, z_scr, w_scr, acc_scr, later_scr):
    qb = pl.program_id(2)
    diag_visits = SB_QROWS // SB_KBLOCK
    visits = diag_visits * (qb + 1)
    last_block = k_ref.shape[0] // SB_KBLOCK - 1
    tri = tri_ref[...]
    lane = lax.broadcasted_iota(jnp.int32, (SB_QROWS, HEAD_PAIR), 1)
    low_half = lane < SB_HEAD_DIM
    q = q_ref[...]
    q_heads = (jnp.where(low_half, q, 0), jnp.where(low_half, 0, q))
    sign_bit = jnp.uint32(0x80000000)
    inv_ln2 = 1.0 / np.log(2.0)
    w_scr[...] = jnp.zeros_like(w_scr)
    acc_scr[...] = jnp.zeros_like(acc_scr)
    later_scr[...] = jnp.zeros_like(later_scr)

    def key_rows(kb):
        return pl.ds(pl.multiple_of(kb * SB_KBLOCK, SB_KBLOCK), SB_KBLOCK)

    def scores(kb, slot, first_row=0):
        kblk = k_ref[key_rows(kb), :]
        for h in range(2):
            z_scr[slot, h, first_row:, :] = _dot_nt(q_heads[h][first_row:], kblk)

    def visit(v, slot, masked, first_row=0):
        kb = visits - 1 - v
        scores(jnp.maximum(kb - 1, 0), 1 - slot)
        vprev = v_ref[key_rows(jnp.minimum(kb + 1, last_block)), :]
        for h in range(2):
            acc_scr[h] += _dot(w_scr[h], vprev)
        if masked:
            t_idx = lax.broadcasted_iota(jnp.int32, (SB_STRIP, SB_KBLOCK), 0) + qb * SB_QROWS
            s_idx = lax.broadcasted_iota(jnp.int32, (SB_STRIP, SB_KBLOCK), 1) + kb * SB_KBLOCK
        strips = [slice(r, r + SB_STRIP) for r in range(first_row, SB_QROWS, SB_STRIP)]
        incls = []
        for h in range(2):
            sps = []
            for rows in strips:
                zr = z_scr[slot, h, rows, :]
                neg_abs = lax.bitcast_convert_type(lax.bitcast_convert_type(zr, jnp.uint32) | sign_bit, F32)
                sp = jnp.maximum(zr, 0.0) + jnp.log(1.0 + jnp.exp2(neg_abs)) * inv_ln2
                if masked:
                    sp = jnp.where(s_idx < t_idx + rows.start, sp, 0.0)
                sps.append(sp.astype(BF16))
            incls.append(_dot(jnp.concatenate(sps, axis=0), tri))
        for h in range(2):
            for rows in strips:
                incl = incls[h][rows.start - first_row:rows.stop - first_row]
                w = jnp.exp2(z_scr[slot, h, rows, :] - incl - later_scr[h, rows, :])
                if masked:
                    w = jnp.where(s_idx < t_idx + rows.start, w, 0.0)
                w_scr[h, rows, :] = w.astype(BF16)
                later_scr[h, rows, :] += incl[:, 0:1]

    def finished(v):
        return jnp.logical_or(v >= visits, jnp.min(later_scr[...]) >= SB_DEAD_LOG2).astype(jnp.int32)

    scores(visits - 1, 0, first_row=SB_QROWS - SB_KBLOCK)
    visit(0, 0, True, first_row=SB_QROWS - SB_KBLOCK)
    visit(1, 1, True)

    def step(carry):
        v, _ = carry
        visit(v, 0, False)
        visit(v + 1, 1, False)
        return v + 2, finished(v + 2)

    assert diag_visits == 2
    v_end, _ = lax.while_loop(lambda carry: carry[1] == 0, step,
                              (jnp.int32(diag_visits), (visits <= diag_visits).astype(jnp.int32)))
    pending = v_ref[key_rows(visits - v_end), :]
    out = [acc_scr[h] + _dot(w_scr[h], pending) for h in range(2)]
    o_ref[...] = jnp.where(low_half, out[0], out[1]).astype(BF16)


def _stick_breaking(q, k, v, batch, seq):
    tokens = q.shape[0]
    blocks = seq // SB_QROWS
    pairs = D_MODEL // HEAD_PAIR
    j = np.arange(SB_KBLOCK)
    tri = jnp.asarray(j[:, None] >= j[None, :], BF16)
    qo_spec = pl.BlockSpec((SB_QROWS, HEAD_PAIR), lambda b, h, i: (b * blocks + i, h))
    kv_spec = pl.BlockSpec((seq, HEAD_PAIR), lambda b, h, i: (b, h))
    return pl.pallas_call(
        _sb_kernel,
        grid=(batch, pairs, blocks),
        in_specs=[qo_spec, kv_spec, kv_spec, _resident((SB_KBLOCK, SB_KBLOCK))],
        out_specs=qo_spec,
        out_shape=jax.ShapeDtypeStruct((tokens, D_MODEL), BF16),
        scratch_shapes=[pltpu.VMEM((2, 2, SB_QROWS, SB_KBLOCK), F32),
                        pltpu.VMEM((2, SB_QROWS, SB_KBLOCK), BF16),
                        pltpu.VMEM((2, SB_QROWS, HEAD_PAIR), F32),
                        pltpu.VMEM((2, SB_QROWS, 1), F32)],
        compiler_params=_params("parallel", "parallel", "arbitrary"),
        name="stick_breaking_attention",
    )(q, k, v, tri)


def _post_kernel(o_ref, x_ref, mod_ref, gain_ref, wo_ref, w1_ref, w2_ref, out_ref):
    gate1 = mod_ref[0, 2:3, :]
    shift2, scale2, gate2 = mod_ref[0, 3:4, :], mod_ref[0, 4:5, :], mod_ref[0, 5:6, :]
    x1 = x_ref[...] + gate1 * _dot(o_ref[...], wo_ref[...])
    h = _modulated_norm(x1, gain_ref[...], scale2, shift2).astype(BF16)
    acc = jnp.zeros_like(x1)
    for c in range(FFN_HIDDEN // FFN_SLAB):
        t = _dot(h, w1_ref[:, c * FFN_SLAB:(c + 1) * FFN_SLAB])
        t = jnp.square(jnp.maximum(t, 0.0)).astype(BF16)
        acc = acc + _dot(t, w2_ref[c * FFN_SLAB:(c + 1) * FFN_SLAB, :])
    out_ref[...] = x1 + gate2 * acc


def _post(o, x2, mod, gain, w_out, w1, w2, seq):
    tokens = x2.shape[0]
    grid, x_spec, mod_spec = _row_specs(tokens, seq, TM_PROJ)
    width = o.shape[1]
    out_spec, out_shape = _row_out(tokens, D_MODEL, TM_PROJ, F32)
    return pl.pallas_call(
        _post_kernel,
        grid=grid,
        in_specs=[pl.BlockSpec((TM_PROJ, width), lambda i: (i, 0)), x_spec, mod_spec,
                  _resident((1, D_MODEL)), _resident((width, D_MODEL)),
                  _resident((D_MODEL, FFN_HIDDEN)), _resident((FFN_HIDDEN, D_MODEL))],
        out_specs=out_spec, out_shape=out_shape,
        compiler_params=_params("parallel"),
        name="outproj_mlp",
    )(o, x2, mod, gain.reshape(1, D_MODEL), w_out.astype(BF16), w1.astype(BF16), w2.astype(BF16))


def kernel(x, c, ada_w, ada_b, norm_mix, norm_ffn, ffn_w1, ffn_w2, a_w_in, a_q_gain, a_k_gain, a_rel_bias, a_w_out, b_w_in, b_v_gain, b_w_s, b_b_s, b_w_out, c_w_in, c_w_gate_up, c_b_gate, c_o_gain, c_w_out, d_w_in, d_w_out):
    batch, seq, _ = x.shape
    x2 = x.reshape(batch * seq, D_MODEL)
    mods = _modulation(c, ada_w, ada_b)
    for i in range(DEPTH):
        m, j = i % N_MIXERS, i // N_MIXERS
        mod = mods[i]
        if m == 0:
            q, k, v = _inproj_a(x2, mod, norm_mix[i], a_w_in[j], a_q_gain[j], a_k_gain[j], seq)
            o = _attn_a(q, k, v, a_rel_bias[j], batch, seq)
            w_out = a_w_out[j]
        elif m == 1:
            o = _mixb(x2, mod, norm_mix[i], b_w_in[j], b_v_gain[j], b_w_s[j], b_b_s[j], seq)
            w_out = b_w_out[j]
        elif m == 2:
            q, k, v, r, la = _inproj_c(x2, mod, norm_mix[i], c_w_in[j], c_w_gate_up[j], c_b_gate[j], seq)
            o = _gla(q, k, v, r, la, c_o_gain[j], batch, seq)
            w_out = c_w_out[j]
        else:
            q, k, v = _inproj_d(x2, mod, norm_mix[i], d_w_in[j], seq)
            o = _stick_breaking(q, k, v, batch, seq)
            w_out = d_w_out[j]
        x2 = _post(o, x2, mod, norm_ffn[i], w_out, ffn_w1[i], ffn_w2[i], seq)
    return x2.reshape(batch, seq, D_MODEL)
```

```python
import jax
import jax.numpy as jnp
import numpy as np
from jax import lax
from jax.experimental import pallas as pl
from jax.experimental.pallas import tpu as pltpu

F32 = jnp.float32
BF16 = jnp.bfloat16

D_MODEL = 1024
DEPTH = 4
N_MIXERS = 4
CHUNK = 64
RMS_EPS = 1e-6

A_HEADS = 16
A_HEAD_DIM = 64
A_LEFT_CHUNKS = 8
A_MAX_REL = 256
A_BAND = (A_LEFT_CHUNKS + 1) * CHUNK
A_BAND_PAD = 640
A_GROUP = 8 * CHUNK
A_INTERLEAVE = 8

SG_FF = 6 * D_MODEL
SG_HALF = SG_FF // 2
SG_GROUPS = 8
SG_WINDOW = 128
SG_GROUP_WIDTH = SG_HALF // SG_GROUPS

GLA_HEADS = 4
GLA_KEY_WIDTH = D_MODEL // 2
GLA_DK = 128
GLA_DV = 256
GLA_GATE_RANK = 16
GLA_GATE_PAD = 128
GLA_TAU = 16.0
GLA_LEVELS = 6
GLA_STEP_CHUNKS = 8
GLA_UNROLL = 4

SB_HEADS = 16
SB_HEAD_DIM = 64
SB_QROWS = 512
SB_KBLOCK = 256
SB_STRIP = 64
SB_DEAD_LOG2 = 160.0
LOG2_E = float(np.log2(np.e))
SB_Q_SCALE = SB_HEAD_DIM ** -0.5 * LOG2_E
A_Q_SCALE = A_HEAD_DIM ** -0.5 * LOG2_E

FFN_HIDDEN = 4 * D_MODEL
FFN_SLAB = 1024

LANES = 128
SUBLANES = 8
MXU_TILE = 256
PROJ_SLAB = 512
MOD_COLS = 1536
HEAD_PAIR = LANES
VMEM_LIMIT_BYTES = 56 * 1024 * 1024

TM_PROJ = 512
TM_MIXB = 256

_NT = (((1,), (1,)), ((), ()))
_TN = (((0,), (0,)), ((), ()))


def _params(*semantics):
    return pltpu.CompilerParams(dimension_semantics=semantics,
                                vmem_limit_bytes=VMEM_LIMIT_BYTES)


def _dot(a, b):
    return jnp.dot(a, b, preferred_element_type=F32)


def _dot_nt(a, b):
    return lax.dot_general(a, b, _NT, preferred_element_type=F32)


def _resident(shape):
    zeros = (0,) * len(shape)
    return pl.BlockSpec(shape, lambda *_: zeros, pipeline_mode=pl.Buffered(1))


def _gelu(x):
    return 0.5 * x * (1.0 + lax.erf(x * (2.0 ** -0.5)))


def _modulated_norm(x, gain, scale, shift):
    ms = jnp.mean(x * x, axis=-1, keepdims=True)
    y = x * lax.rsqrt(ms + RMS_EPS) * gain
    return y * (1.0 + scale) + shift


def _mod_kernel(c_ref, w_ref, b_ref, o_ref):
    cond = jax.nn.silu(c_ref[...])
    o_ref[0] = jnp.dot(cond, w_ref[0], precision=lax.Precision.HIGHEST,
                       preferred_element_type=F32) + b_ref[0]


def _modulation(c, ada_w, ada_b):
    batch = c.shape[0]
    rows = SUBLANES
    c_pad = jnp.zeros((rows, D_MODEL), F32).at[:batch].set(c)
    tn = MOD_COLS
    out = pl.pallas_call(
        _mod_kernel,
        grid=(DEPTH, 6 * D_MODEL // tn),
        in_specs=[
            pl.BlockSpec((rows, D_MODEL), lambda i, j: (0, 0)),
            pl.BlockSpec((1, D_MODEL, tn), lambda i, j: (i, 0, j)),
            pl.BlockSpec((1, 1, tn), lambda i, j: (i, 0, j)),
        ],
        out_specs=pl.BlockSpec((1, rows, tn), lambda i, j: (i, 0, j)),
        out_shape=jax.ShapeDtypeStruct((DEPTH, rows, 6 * D_MODEL), F32),
        compiler_params=_params("parallel", "parallel"),
        name="adaln_modulation",
    )(c_pad, ada_w, ada_b.reshape(DEPTH, 1, 6 * D_MODEL))
    return out[:, :batch].reshape(DEPTH, batch, 6, D_MODEL)


def _normed_input(x_ref, mod_ref, gain_ref):
    shift, scale = mod_ref[0, 0:1, :], mod_ref[0, 1:2, :]
    return _modulated_norm(x_ref[...], gain_ref[...], scale, shift).astype(BF16)


def _inproj_a_kernel(x_ref, mod_ref, gain_ref, w_ref, qg_ref, kg_ref, seg_ref,
                     q_ref, k_ref, v_ref):
    h = _normed_input(x_ref, mod_ref, gain_ref)
    seg = seg_ref[...]

    def head_norm(y, gain):
        ssq = _dot((y * y).astype(BF16), seg)
        return y * lax.rsqrt(ssq * (1.0 / A_HEAD_DIM) + RMS_EPS) * gain

    width = seg.shape[0]
    slabs = [(part, c) for c in range(D_MODEL // width) for part in range(3)]

    def project(part, c):
        start = part * D_MODEL + c * width
        return _dot(h, w_ref[:, start:start + width])

    y_next = project(*slabs[0])
    for i, (part, c) in enumerate(slabs):
        y = y_next
        if i + 1 < len(slabs):
            y_next = project(*slabs[i + 1])
        cols = slice(c * width, (c + 1) * width)
        if part == 0:
            q_ref[:, cols] = (head_norm(y, qg_ref[:, cols]) * A_Q_SCALE).astype(BF16)
        elif part == 1:
            k_ref[:, cols] = head_norm(y, kg_ref[:, cols]).astype(BF16)
        else:
            v_ref[:, cols] = y.astype(BF16)


def _inproj_d_kernel(x_ref, mod_ref, gain_ref, w_ref, q_ref, k_ref, v_ref):
    h = _normed_input(x_ref, mod_ref, gain_ref)
    width = PROJ_SLAB
    for c in range(D_MODEL // width):
        cols = slice(c * width, (c + 1) * width)
        yq = _dot(h, w_ref[:, c * width:(c + 1) * width])
        q_ref[:, cols] = (yq * SB_Q_SCALE).astype(BF16)
        yk = _dot(h, w_ref[:, D_MODEL + c * width:D_MODEL + (c + 1) * width])
        k_ref[:, cols] = yk.astype(BF16)
        yv = _dot(h, w_ref[:, 2 * D_MODEL + c * width:2 * D_MODEL + (c + 1) * width])
        v_ref[:, cols] = yv.astype(BF16)


def _inproj_c_kernel(x_ref, mod_ref, gain_ref, w_ref, wg_ref, bg_ref,
                     q_ref, k_ref, v_ref, r_ref, la_ref):
    h = _normed_input(x_ref, mod_ref, gain_ref)
    kw = GLA_KEY_WIDTH
    a = _dot(h, w_ref[:, 6 * kw:6 * kw + GLA_GATE_PAD]).astype(BF16)
    y_next = _dot(h, w_ref[:, 0:kw])
    gate = _dot(a, wg_ref[...]) + bg_ref[...]
    la_ref[...] = jax.nn.log_sigmoid(gate) * (1.0 / GLA_TAU)
    outs = [(q_ref, slice(0, kw)), (k_ref, slice(0, kw)), (v_ref, slice(0, kw)), (v_ref, slice(kw, 2 * kw)),
            (r_ref, slice(0, kw)), (r_ref, slice(kw, 2 * kw))]
    for i, (ref, cols) in enumerate(outs):
        y = y_next
        if i + 1 < len(outs):
            y_next = _dot(h, w_ref[:, (i + 1) * kw:(i + 2) * kw])
        if i == 0:
            y = y * (GLA_DK ** -0.5)
        ref[:, cols] = y.astype(BF16)


def _row_specs(tokens, seq, tm):
    tiles_per_batch = seq // tm
    x_spec = pl.BlockSpec((tm, D_MODEL), lambda i: (i, 0))
    mod_spec = pl.BlockSpec((1, 6, D_MODEL), lambda i: (i // tiles_per_batch, 0, 0))
    return (tokens // tm,), x_spec, mod_spec


def _row_out(tokens, width, tm, dtype=BF16):
    return (pl.BlockSpec((tm, width), lambda i: (i, 0)),
            jax.ShapeDtypeStruct((tokens, width), dtype))


def _inproj_a(x2, mod, gain, w_in, q_gain, k_gain, seq):
    tokens = x2.shape[0]
    grid, x_spec, mod_spec = _row_specs(tokens, seq, TM_PROJ)
    seg_width = MXU_TILE
    seg_id = np.arange(seg_width) // A_HEAD_DIM
    seg = jnp.asarray(seg_id[:, None] == seg_id[None, :], BF16)
    specs, shapes = zip(*[_row_out(tokens, D_MODEL, TM_PROJ)] * 3)
    return pl.pallas_call(
        _inproj_a_kernel,
        grid=grid,
        in_specs=[x_spec, mod_spec, _resident((1, D_MODEL)), _resident((D_MODEL, 3 * D_MODEL)),
                  _resident((1, D_MODEL)), _resident((1, D_MODEL)),
                  _resident((seg_width, seg_width))],
        out_specs=list(specs), out_shape=list(shapes),
        compiler_params=_params("parallel"),
        name="inproj_relpos_attention",
    )(x2, mod, gain.reshape(1, D_MODEL), w_in.astype(BF16),
      jnp.tile(q_gain, A_HEADS).reshape(1, D_MODEL), jnp.tile(k_gain, A_HEADS).reshape(1, D_MODEL), seg)


def _inproj_d(x2, mod, gain, w_in, seq):
    tokens = x2.shape[0]
    grid, x_spec, mod_spec = _row_specs(tokens, seq, TM_PROJ)
    specs, shapes = zip(*[_row_out(tokens, D_MODEL, TM_PROJ)] * 3)
    return pl.pallas_call(
        _inproj_d_kernel,
        grid=grid,
        in_specs=[x_spec, mod_spec, _resident((1, D_MODEL)), _resident((D_MODEL, 3 * D_MODEL))],
        out_specs=list(specs), out_shape=list(shapes),
        compiler_params=_params("parallel"),
        name="inproj_stick_breaking",
    )(x2, mod, gain.reshape(1, D_MODEL), w_in.astype(BF16))


def _inproj_c(x2, mod, gain, w_in, w_gate_up, b_gate, seq):
    tokens = x2.shape[0]
    grid, x_spec, mod_spec = _row_specs(tokens, seq, TM_PROJ)
    main = 2 * GLA_KEY_WIDTH + 2 * D_MODEL
    w_pad = jnp.zeros((D_MODEL, main + GLA_GATE_PAD), BF16).at[:, :main + GLA_GATE_RANK].set(
        w_in.astype(BF16))
    wg_pad = jnp.zeros((GLA_GATE_PAD, GLA_KEY_WIDTH), BF16).at[:GLA_GATE_RANK].set(
        w_gate_up.astype(BF16))
    outs = [_row_out(tokens, GLA_KEY_WIDTH, TM_PROJ), _row_out(tokens, GLA_KEY_WIDTH, TM_PROJ),
            _row_out(tokens, D_MODEL, TM_PROJ), _row_out(tokens, D_MODEL, TM_PROJ),
            _row_out(tokens, GLA_KEY_WIDTH, TM_PROJ, F32)]
    specs, shapes = zip(*outs)
    return pl.pallas_call(
        _inproj_c_kernel,
        grid=grid,
        in_specs=[x_spec, mod_spec, _resident((1, D_MODEL)),
                  _resident((D_MODEL, main + GLA_GATE_PAD)),
                  _resident((GLA_GATE_PAD, GLA_KEY_WIDTH)), _resident((1, GLA_KEY_WIDTH))],
        out_specs=list(specs), out_shape=list(shapes),
        compiler_params=_params("parallel"),
        name="inproj_gla",
    )(x2, mod, gain.reshape(1, D_MODEL), w_pad, wg_pad, b_gate.reshape(1, GLA_KEY_WIDTH))


def _attn_a_kernel(q_ref, kp_ref, kc_ref, vp_ref, vc_ref, bias_ref, o_ref, kwin, vwin, bias_scr):
    block = pl.program_id(2)
    kwin[0:A_GROUP, :] = kp_ref[...]
    kwin[A_GROUP:2 * A_GROUP, :] = kc_ref[...]
    vwin[0:A_GROUP, :] = vp_ref[...]
    vwin[A_GROUP:2 * A_GROUP, :] = vc_ref[...]
    tail = jnp.zeros((A_BAND_PAD - A_BAND, HEAD_PAIR), BF16)
    kwin[2 * A_GROUP:, :] = tail
    vwin[2 * A_GROUP:, :] = tail

    lane = lax.broadcasted_iota(jnp.int32, (CHUNK, HEAD_PAIR), 1)
    low_half = lane < A_HEAD_DIM
    chunks = A_GROUP // CHUNK

    @pl.when(block == 0)
    def _():
        col = lax.broadcasted_iota(jnp.int32, (2 * CHUNK, A_BAND_PAD), 1)
        for c in range(chunks):
            bias_scr[c] = jnp.where(col >= A_GROUP - c * CHUNK, bias_ref[0], -jnp.inf)

    @pl.when(block == 1)
    def _():
        for c in range(chunks):
            bias_scr[c] = bias_ref[0]

    for group in range(0, chunks, A_INTERLEAVE):
        members = range(group, group + A_INTERLEAVE)
        scores = []
        for c in members:
            qc = q_ref[c * CHUNK:(c + 1) * CHUNK, :]
            q2 = jnp.concatenate([jnp.where(low_half, qc, 0), jnp.where(low_half, 0, qc)], axis=0)
            scores.append(_dot_nt(q2, kwin[c * CHUNK:c * CHUNK + A_BAND_PAD, :]) + bias_scr[c])
        maxes = [jnp.max(s, axis=-1, keepdims=True) for s in scores]
        probs, denoms = [], []
        for s, m in zip(scores, maxes):
            p = jnp.exp2(s - m)
            denoms.append(jnp.sum(p, axis=-1, keepdims=True))
            probs.append(p.astype(BF16))
        outs = [_dot(p, vwin[c * CHUNK:c * CHUNK + A_BAND_PAD, :]) / d
                for c, p, d in zip(members, probs, denoms)]
        for c, out in zip(members, outs):
            o_ref[c * CHUNK:(c + 1) * CHUNK, :] = jnp.where(
                low_half, out[0:CHUNK], out[CHUNK:2 * CHUNK]).astype(BF16)


def _relpos_bias_table(rel_bias):
    rb = rel_bias.astype(F32) * LOG2_E
    far = A_MAX_REL + CHUNK - 1
    ext = jnp.concatenate([jnp.broadcast_to(rb[:, far:far + 1], (A_HEADS, far + 1)),
                           rb[:, far - 1::-1]], axis=1)
    table = jnp.stack([ext[:, CHUNK - 1 - i:CHUNK - 1 - i + A_BAND] for i in range(CHUNK)], axis=1)
    table = jnp.pad(table, ((0, 0), (0, 0), (0, A_BAND_PAD - A_BAND)), constant_values=-jnp.inf)
    return table.reshape(A_HEADS // 2, 2 * CHUNK, A_BAND_PAD)


def _attn_a(q, k, v, rel_bias, batch, seq):
    tokens = q.shape[0]
    blocks = seq // A_GROUP
    pairs = D_MODEL // HEAD_PAIR
    cur = lambda b, h, j: (b * blocks + j, h)
    prev = lambda b, h, j: (b * blocks + jnp.maximum(j - 1, 0), h)
    blk = (A_GROUP, HEAD_PAIR)
    win_rows = 2 * A_GROUP + A_BAND_PAD - A_BAND
    return pl.pallas_call(
        _attn_a_kernel,
        grid=(batch, pairs, blocks),
        in_specs=[pl.BlockSpec(blk, cur), pl.BlockSpec(blk, prev), pl.BlockSpec(blk, cur),
                  pl.BlockSpec(blk, prev), pl.BlockSpec(blk, cur),
                  pl.BlockSpec((1, 2 * CHUNK, A_BAND_PAD), lambda b, h, j: (h, 0, 0))],
        out_specs=pl.BlockSpec(blk, cur),
        out_shape=jax.ShapeDtypeStruct((tokens, D_MODEL), BF16),
        scratch_shapes=[pltpu.VMEM((win_rows, HEAD_PAIR), BF16),
                        pltpu.VMEM((win_rows, HEAD_PAIR), BF16),
                        pltpu.VMEM((A_GROUP // CHUNK, 2 * CHUNK, A_BAND_PAD), F32)],
        compiler_params=_params("parallel", "parallel", "arbitrary"),
        name="relpos_band_attention",
    )(q, k, k, v, v, _relpos_bias_table(rel_bias))


def _mixb_kernel(x_ref, mod_ref, gain_ref, w_ref, vg_ref, ws_ref, bs_ref, o_ref, v_scr):
    h = _normed_input(x_ref, mod_ref, gain_ref)
    tm = h.shape[0]
    slab = PROJ_SLAB
    ssq = jnp.zeros((tm, 1), F32)
    for c in range(SG_HALF // slab):
        zc = _gelu(_dot(h, w_ref[:, SG_HALF + c * slab:SG_HALF + (c + 1) * slab]))
        ssq = ssq + jnp.sum(zc * zc, axis=-1, keepdims=True)
        v_scr[:, c * slab:(c + 1) * slab] = zc
    inv_rms = lax.rsqrt(ssq * (1.0 / SG_HALF) + RMS_EPS)

    row = lax.broadcasted_iota(jnp.int32, (SG_WINDOW, SG_WINDOW), 0) // CHUNK
    colc = lax.broadcasted_iota(jnp.int32, (SG_WINDOW, SG_WINDOW), 1) // CHUNK
    block_lower = row >= colc
    gw = SG_GROUP_WIDTH
    for g in range(SG_GROUPS):
        cols = slice(g * gw, (g + 1) * gw)
        w_s = jnp.where(block_lower, ws_ref[g], 0.0).astype(BF16)
        vn = (v_scr[:, cols] * inv_rms * vg_ref[:, cols]).astype(BF16)
        u = _gelu(_dot(h, w_ref[:, g * gw:(g + 1) * gw]))
        for w in range(tm // SG_WINDOW):
            rows = slice(w * SG_WINDOW, (w + 1) * SG_WINDOW)
            vm = _dot(w_s, vn[rows]) + bs_ref[g]
            o_ref[rows, cols] = (u[rows] * vm).astype(BF16)


def _mixb(x2, mod, gain, w_in, v_gain, w_s, b_s, seq):
    tokens = x2.shape[0]
    grid, x_spec, mod_spec = _row_specs(tokens, seq, TM_MIXB)
    bias = jnp.broadcast_to(b_s.astype(F32)[:, :, None], (SG_GROUPS, SG_WINDOW, SG_GROUP_WIDTH))
    out_spec, out_shape = _row_out(tokens, SG_HALF, TM_MIXB)
    return pl.pallas_call(
        _mixb_kernel,
        grid=grid,
        in_specs=[x_spec, mod_spec, _resident((1, D_MODEL)), _resident((D_MODEL, SG_FF)),
                  _resident((1, SG_HALF)), _resident((SG_GROUPS, SG_WINDOW, SG_WINDOW)),
                  _resident((SG_GROUPS, SG_WINDOW, SG_GROUP_WIDTH))],
        out_specs=out_spec, out_shape=out_shape,
        scratch_shapes=[pltpu.VMEM((TM_MIXB, SG_HALF), F32)],
        compiler_params=_params("parallel"),
        name="gmlp_spatial_gating",
    )(x2, mod, gain.reshape(1, D_MODEL), w_in.astype(BF16), v_gain.reshape(1, SG_HALF),
      w_s.astype(F32), bias)


def _gla_decay_matrix():
    r = np.arange(CHUNK)[:, None]
    i = np.arange(CHUNK)[None, :]
    parts = [i <= r, i > r]
    for level in range(GLA_LEVELS):
        n = (CHUNK // 2) >> level
        ref = (r // (2 * n)) * 2 * n + n
        upper = (r % (2 * n)) >= n
        parts.append(np.where(upper, (i > ref) & (i <= r), (i > r) & (i <= ref)))
    m = np.concatenate(parts, axis=0)
    return jnp.asarray(np.concatenate([m, m], axis=1), BF16)


def _gla_level_map():
    t = np.arange(CHUNK)[:, None]
    s = np.arange(CHUNK)[None, :]
    out = np.full((CHUNK, CHUNK), -1, np.int32)
    out[t == s] = 0
    for level in range(GLA_LEVELS):
        n = (CHUNK // 2) >> level
        hit = (t // (2 * n) == s // (2 * n)) & (t % (2 * n) >= n) & (s % (2 * n) < n)
        out[hit] = 1 + level
    return jnp.asarray(out)


def _gla_kernel(q_ref, k_ref, v_ref, r_ref, la_ref, dm_ref, lvl_ref, og_ref, o_ref, state):
    @pl.when(pl.program_id(1) == 0)
    def _():
        state[...] = jnp.zeros_like(state)

    dm = dm_ref[...]
    lvl = lvl_ref[...]

    heads = range(GLA_HEADS)
    kcols = [slice(h * GLA_DK, (h + 1) * GLA_DK) for h in heads]
    vcols = [slice(h * GLA_DV, (h + 1) * GLA_DV) for h in heads]

    def group(g, carry):
        base = g * (GLA_UNROLL * CHUNK)
        chunk_rows = [pl.ds(pl.multiple_of(base + u * CHUNK, CHUNK), CHUNK) for u in range(GLA_UNROLL)]
        decays = []
        for rows in chunk_rows:
            la = la_ref[rows, :]
            la_hi = la.astype(BF16)
            la_lo = (la - la_hi.astype(F32)).astype(BF16)
            decays.append(jnp.exp(_dot(dm, jnp.concatenate([la_hi, la_lo], axis=0))))
        units = [(u, h) for u in range(GLA_UNROLL) for h in heads]
        qs = [q_ref[chunk_rows[u], kcols[h]].astype(F32) for u, h in units]
        ks = [k_ref[chunk_rows[u], kcols[h]].astype(F32) for u, h in units]
        pairs = []
        for (u, h), qh, kh in zip(units, qs, ks):
            level_pairs = [_dot_nt(qh.astype(BF16), kh.astype(BF16))]
            for level in range(GLA_LEVELS):
                e = decays[u][(2 + level) * CHUNK:(3 + level) * CHUNK, kcols[h]]
                level_pairs.append(_dot_nt((qh * e).astype(BF16), (kh * e).astype(BF16)))
            pairs.append(level_pairs)
        intra = []
        for (u, h), level_pairs in zip(units, pairs):
            att = jnp.where(lvl == 0, level_pairs[0], 0.0)
            for level in range(GLA_LEVELS):
                att = jnp.where(lvl == 1 + level, level_pairs[1 + level], att)
            intra.append(_dot(att.astype(BF16), v_ref[chunk_rows[u], vcols[h]]))
        states = [state[h] for h in heads]
        for idx, (u, h) in enumerate(units):
            rows, kc, vc = chunk_rows[u], kcols[h], vcols[h]
            decay = decays[u]
            o = intra[idx] + _dot_nt((qs[idx] * decay[0:CHUNK, kc]).astype(BF16), states[h].astype(BF16))
            k_dec = (ks[idx] * decay[CHUNK:2 * CHUNK, kc]).astype(BF16)
            states[h] = (states[h] * decay[CHUNK - 1:CHUNK, kc]
                         + lax.dot_general(v_ref[rows, vc], k_dec, _TN, preferred_element_type=F32))
            ms = jnp.mean(o * o, axis=-1, keepdims=True)
            on = o * lax.rsqrt(ms + RMS_EPS) * og_ref[:, vc]
            o_ref[rows, vc] = (jax.nn.silu(r_ref[rows, vc].astype(F32)) * on).astype(BF16)
        for h in heads:
            state[h] = states[h]
        return carry

    lax.fori_loop(0, GLA_STEP_CHUNKS // GLA_UNROLL, group, 0)


def _gla(q, k, v, r, la, o_gain, batch, seq):
    tokens = q.shape[0]
    step = GLA_STEP_CHUNKS * CHUNK
    blocks = seq // step
    idx = lambda b, j: (b * blocks + j, 0)
    return pl.pallas_call(
        _gla_kernel,
        grid=(batch, blocks),
        in_specs=[pl.BlockSpec((step, GLA_KEY_WIDTH), idx), pl.BlockSpec((step, GLA_KEY_WIDTH), idx),
                  pl.BlockSpec((step, D_MODEL), idx), pl.BlockSpec((step, D_MODEL), idx),
                  pl.BlockSpec((step, GLA_KEY_WIDTH), idx),
                  _resident((8 * CHUNK, 2 * CHUNK)), _resident((CHUNK, CHUNK)), _resident((1, D_MODEL))],
        out_specs=pl.BlockSpec((step, D_MODEL), idx),
        out_shape=jax.ShapeDtypeStruct((tokens, D_MODEL), BF16),
        scratch_shapes=[pltpu.VMEM((GLA_HEADS, GLA_DV, GLA_DK), F32)],
        compiler_params=_params("parallel", "arbitrary"),
        name="gla_scan",
    )(q, k, v, r, la, _gla_decay_matrix(), _gla_level_map(),
      jnp.tile(o_gain, GLA_HEADS).reshape(1, D_MODEL))


def _sb_kernel(q_ref, k_ref, v_ref, tri_ref, o_ref, z_scr, w_scr, acc_scr, later_scr):
    qb = pl.program_id(2)
    diag_visits = SB_QROWS // SB_KBLOCK
    visits = diag_visits * (qb + 1)
    last_block = k_ref.shape[0] // SB_KBLOCK - 1
    tri = tri_ref[...]
    lane = lax.broadcasted_iota(jnp.int32, (SB_QROWS, HEAD_PAIR), 1)
    low_half = lane < SB_HEAD_DIM
    q = q_ref[...]
    q_heads = (jnp.where(low_half, q, 0), jnp.where(low_half, 0, q))
    sign_bit = jnp.uint32(0x80000000)
    inv_ln2 = 1.0 / np.log(2.0)
    w_scr[...] = jnp.zeros_like(w_scr)
    acc_scr[...] = jnp.zeros_like(acc_scr)
    later_scr[...] = jnp.zeros_like(later_scr)

    def key_rows(kb):
        return pl.ds(pl.multiple_of(kb * SB_KBLOCK, SB_KBLOCK), SB_KBLOCK)

    def scores(kb, slot, first_row=0):
        kblk = k_ref[key_rows(kb), :]
        for h in range(2):
            z_scr[slot, h, first_row:, :] = _dot_nt(q_heads[h][first_row:], kblk)

    def visit(v, slot, masked, first_row=0):
        kb = visits - 1 - v
        scores(jnp.maximum(kb - 1, 0), 1 - slot)
        vprev = v_ref[key_rows(jnp.minimum(kb + 1, last_block)), :]
        for h in range(2):
            acc_scr[h] += _dot(w_scr[h], vprev)
        if masked:
            t_idx = lax.broadcasted_iota(jnp.int32, (SB_STRIP, SB_KBLOCK), 0) + qb * SB_QROWS
            s_idx = lax.broadcasted_iota(jnp.int32, (SB_STRIP, SB_KBLOCK), 1) + kb * SB_KBLOCK
        strips = [slice(r, r + SB_STRIP) for r in range(first_row, SB_QROWS, SB_STRIP)]
        incls = []
        for h in range(2):
            sps = []
            for rows in strips:
                zr = z_scr[slot, h, rows, :]
                neg_abs = lax.bitcast_convert_type(lax.bitcast_convert_type(zr, jnp.uint32) | sign_bit, F32)
                sp = jnp.maximum(zr, 0.0) + jnp.log(1.0 + jnp.exp2(neg_abs)) * inv_ln2
                if masked:
                    sp = jnp.where(s_idx < t_idx + rows.start, sp, 0.0)
                sps.append(sp.astype(BF16))
            incls.append(_dot(jnp.concatenate(sps, axis=0), tri))
        low = None
        for h in range(2):
            for rows in strips:
                incl = incls[h][rows.start - first_row:rows.stop - first_row]
                later = later_scr[h, rows, :]
                w = jnp.exp2(z_scr[slot, h, rows, :] - incl - later)
                if masked:
                    w = jnp.where(s_idx < t_idx + rows.start, w, 0.0)
                w_scr[h, rows, :] = w.astype(BF16)
                later = later + incl[:, 0:1]
                later_scr[h, rows, :] = later
                low = later if low is None else jnp.minimum(low, later)
        return low

    def finished(v, low):
        return jnp.logical_or(v >= visits, jnp.min(low) >= SB_DEAD_LOG2).astype(jnp.int32)

    scores(visits - 1, 0, first_row=SB_QROWS - SB_KBLOCK)
    visit(0, 0, True, first_row=SB_QROWS - SB_KBLOCK)
    visit(1, 1, True)

    def step(carry):
        v, _ = carry
        visit(v, 0, False)
        low = visit(v + 1, 1, False)
        return v + 2, finished(v + 2, low)

    assert diag_visits == 2
    v_end, _ = lax.while_loop(lambda carry: carry[1] == 0, step,
                              (jnp.int32(diag_visits), (visits <= diag_visits).astype(jnp.int32)))
    pending = v_ref[key_rows(visits - v_end), :]
    out = [acc_scr[h] + _dot(w_scr[h], pending) for h in range(2)]
    o_ref[...] = jnp.where(low_half, out[0], out[1]).astype(BF16)


def _stick_breaking(q, k, v, batch, seq):
    tokens = q.shape[0]
    blocks = seq // SB_QROWS
    pairs = D_MODEL // HEAD_PAIR
    j = np.arange(SB_KBLOCK)
    tri = jnp.asarray(j[:, None] >= j[None, :], BF16)
    qo_spec = pl.BlockSpec((SB_QROWS, HEAD_PAIR), lambda b, h, i: (b * blocks + i, h))
    kv_spec = pl.BlockSpec((seq, HEAD_PAIR), lambda b, h, i: (b, h))
    return pl.pallas_call(
        _sb_kernel,
        grid=(batch, pairs, blocks),
        in_specs=[qo_spec, kv_spec, kv_spec, _resident((SB_KBLOCK, SB_KBLOCK))],
        out_specs=qo_spec,
        out_shape=jax.ShapeDtypeStruct((tokens, D_MODEL), BF16),
        scratch_shapes=[pltpu.VMEM((2, 2, SB_QROWS, SB_KBLOCK), F32),
                        pltpu.VMEM((2, SB_QROWS, SB_KBLOCK), BF16),
                        pltpu.VMEM((2, SB_QROWS, HEAD_PAIR), F32),
                        pltpu.VMEM((2, SB_QROWS, 1), F32)],
        compiler_params=_params("parallel", "parallel", "arbitrary"),
        name="stick_breaking_attention",
    )(q, k, v, tri)


def _post_kernel(o_ref, x_ref, mod_ref, gain_ref, wo_ref, w1_ref, w2_ref, out_ref):
    gate1 = mod_ref[0, 2:3, :]
    shift2, scale2, gate2 = mod_ref[0, 3:4, :], mod_ref[0, 4:5, :], mod_ref[0, 5:6, :]
    x1 = x_ref[...] + gate1 * _dot(o_ref[...], wo_ref[...])
    h = _modulated_norm(x1, gain_ref[...], scale2, shift2).astype(BF16)
    acc = jnp.zeros_like(x1)
    for c in range(FFN_HIDDEN // FFN_SLAB):
        t = _dot(h, w1_ref[:, c * FFN_SLAB:(c + 1) * FFN_SLAB])
        t = jnp.square(jnp.maximum(t, 0.0)).astype(BF16)
        acc = acc + _dot(t, w2_ref[c * FFN_SLAB:(c + 1) * FFN_SLAB, :])
    out_ref[...] = x1 + gate2 * acc


def _post(o, x2, mod, gain, w_out, w1, w2, seq):
    tokens = x2.shape[0]
    grid, x_spec, mod_spec = _row_specs(tokens, seq, TM_PROJ)
    width = o.shape[1]
    out_spec, out_shape = _row_out(tokens, D_MODEL, TM_PROJ, F32)
    return pl.pallas_call(
        _post_kernel,
        grid=grid,
        in_specs=[pl.BlockSpec((TM_PROJ, width), lambda i: (i, 0)), x_spec, mod_spec,
                  _resident((1, D_MODEL)), _resident((width, D_MODEL)),
                  _resident((D_MODEL, FFN_HIDDEN)), _resident((FFN_HIDDEN, D_MODEL))],
        out_specs=out_spec, out_shape=out_shape,
        compiler_params=_params("parallel"),
        name="outproj_mlp",
    )(o, x2, mod, gain.reshape(1, D_MODEL), w_out.astype(BF16), w1.astype(BF16), w2.astype(BF16))


def kernel(x, c, ada_w, ada_b, norm_mix, norm_ffn, ffn_w1, ffn_w2, a_w_in, a_q_gain, a_k_gain, a_rel_bias, a_w_out, b_w_in, b_v_gain, b_w_s, b_b_s, b_w_out, c_w_in, c_w_gate_up, c_b_gate, c_o_gain, c_w_out, d_w_in, d_w_out):
    batch, seq, _ = x.shape
    x2 = x.reshape(batch * seq, D_MODEL)
    mods = _modulation(c, ada_w, ada_b)
    for i in range(DEPTH):
        m, j = i % N_MIXERS, i // N_MIXERS
        mod = mods[i]
        if m == 0:
            q, k, v = _inproj_a(x2, mod, norm_mix[i], a_w_in[j], a_q_gain[j], a_k_gain[j], seq)
            o = _attn_a(q, k, v, a_rel_bias[j], batch, seq)
            w_out = a_w_out[j]
        elif m == 1:
            o = _mixb(x2, mod, norm_mix[i], b_w_in[j], b_v_gain[j], b_w_s[j], b_b_s[j], seq)
            w_out = b_w_out[j]
        elif m == 2:
            q, k, v, r, la = _inproj_c(x2, mod, norm_mix[i], c_w_in[j], c_w_gate_up[j], c_b_gate[j], seq)
            o = _gla(q, k, v, r, la, c_o_gain[j], batch, seq)
            w_out = c_w_out[j]
        else:
            q, k, v = _inproj_d(x2, mod, norm_mix[i], d_w_in[j], seq)
            o = _stick_breaking(q, k, v, batch, seq)
            w_out = d_w_out[j]
        x2 = _post(o, x2, mod, norm_ffn[i], w_out, ffn_w1[i], ffn_w2[i], seq)
    return x2.reshape(batch, seq, D_MODEL)
```
